```python
import jax, jax.numpy as jnp
from jax import lax
import numpy as np

D_MODEL = 1024
BATCH = 8
SEQ = 4096
DEPTH = 4

CHUNK = 64
N_MIXERS = 2
EPS = 1e-6

GDN_QK_HEADS = 8
GDN_V_HEADS = 16
GDN_HEAD_DIM = 128
GDN_QK_WIDTH = GDN_QK_HEADS * GDN_HEAD_DIM
GDN_V_WIDTH = GDN_V_HEADS * GDN_HEAD_DIM
GDN_CONV_CH = 2 * GDN_QK_WIDTH + GDN_V_WIDTH
GDN_IN_WIDTH = GDN_CONV_CH + GDN_V_WIDTH + 2 * GDN_V_HEADS
CONV_WIDTH = 4

FOX_HEADS = 16
FOX_HEAD_DIM = 64
FOX_WIDTH = FOX_HEADS * FOX_HEAD_DIM
FOX_IN_WIDTH = 4 * FOX_WIDTH + FOX_HEADS
Q_BLOCK = 128

N_LAYERS_A = (DEPTH + 1) // 2
N_LAYERS_B = DEPTH // 2

kernel_name = "hybrid_gdn_fox_adaln_trunk"


def rmsnorm(x, w):
    xf = x.astype(jnp.float32)
    y = xf * lax.rsqrt(jnp.mean(xf * xf, axis=-1, keepdims=True) + EPS)
    return (y * w.astype(jnp.float32)).astype(x.dtype)


def l2norm(x):
    return x * lax.rsqrt(jnp.sum(x * x, axis=-1, keepdims=True) + EPS)


def causal_depthwise_conv(x, w):
    C = x.shape[-1]
    return lax.conv_general_dilated(
        x, w[:, None, :].astype(x.dtype), window_strides=(1,),
        padding=[(CONV_WIDTH - 1, 0)], dimension_numbers=('NWC', 'WIO', 'NWC'),
        feature_group_count=C)


def gated_delta_rule(q, k, v, g, beta):
    B, S, H, DK = q.shape
    DV = v.shape[-1]
    N = S // CHUNK

    def to_chunks(t):
        t = jnp.moveaxis(t, 2, 1)
        return t.reshape(t.shape[:2] + (N, CHUNK) + t.shape[3:])

    q, k, v, g, beta = (to_chunks(t) for t in (q, k, v, g, beta))
    g = jnp.cumsum(g, axis=-1)
    idx = jnp.arange(CHUNK)
    lower = idx[:, None] >= idx[None, :]
    strict = idx[:, None] > idx[None, :]
    decay = jnp.exp(jnp.where(lower, g[..., :, None] - g[..., None, :], -jnp.inf))
    kb = k * beta[..., None]
    vb = v * beta[..., None]
    L = jnp.where(strict, jnp.einsum('bhncd,bhnsd->bhncs', kb, k) * decay, 0.0)
    eye = jnp.eye(CHUNK, dtype=q.dtype)
    T = lax.linalg.triangular_solve(eye + L, jnp.broadcast_to(eye, L.shape),
                                    left_side=True, lower=True, unit_diagonal=True)
    u = T @ vb
    w = T @ (kb * jnp.exp(g)[..., None])
    attn = jnp.einsum('bhncd,bhnsd->bhncs', q, k) * decay
    q_dec = q * jnp.exp(g)[..., None]
    k_dec = k * jnp.exp(g[..., -1:] - g)[..., None]
    g_last = jnp.exp(g[..., -1])
    xs = tuple(jnp.moveaxis(t, 2, 0) for t in (q_dec, k_dec, u, w, attn, g_last))

    def step(state, inp):
        q_n, k_n, u_n, w_n, a_n, gl_n = inp
        v_new = u_n - jnp.einsum('bhcd,bhde->bhce', w_n, state)
        o = (jnp.einsum('bhcd,bhde->bhce', q_n, state)
             + jnp.einsum('bhcs,bhse->bhce', a_n, v_new))
        state = state * gl_n[..., None, None] + jnp.einsum('bhcd,bhce->bhde', k_n, v_new)
        return state, o

    state0 = jnp.zeros((B, H, DK, DV), q.dtype)
    _, o = lax.scan(step, state0, xs)
    o = jnp.moveaxis(o, 0, 2).reshape(B, H, S, DV)
    return jnp.moveaxis(o, 1, 2)


def gdn_mixer(h, w_in, conv_w, A_log, dt_bias, norm_w, w_out):
    B, S, _ = h.shape
    proj = h @ w_in
    qkv, z, b, a = jnp.split(proj, [GDN_CONV_CH, GDN_CONV_CH + GDN_V_WIDTH,
                                    GDN_CONV_CH + GDN_V_WIDTH + GDN_V_HEADS], axis=-1)
    qkv = jax.nn.silu(causal_depthwise_conv(qkv, conv_w))
    q, k, v = jnp.split(qkv, [GDN_QK_WIDTH, 2 * GDN_QK_WIDTH], axis=-1)
    rep = GDN_V_HEADS // GDN_QK_HEADS
    q = l2norm(q.reshape(B, S, GDN_QK_HEADS, GDN_HEAD_DIM).astype(jnp.float32)) * GDN_HEAD_DIM ** -0.5
    k = l2norm(k.reshape(B, S, GDN_QK_HEADS, GDN_HEAD_DIM).astype(jnp.float32))
    q = jnp.repeat(q, rep, axis=2)
    k = jnp.repeat(k, rep, axis=2)
    v = v.reshape(B, S, GDN_V_HEADS, GDN_HEAD_DIM).astype(jnp.float32)
    beta = jax.nn.sigmoid(b.astype(jnp.float32))
    g = -jnp.exp(A_log.astype(jnp.float32)) * jax.nn.softplus(
        a.astype(jnp.float32) + dt_bias.astype(jnp.float32))
    o = gated_delta_rule(q, k, v, g, beta)
    zg = jax.nn.silu(z.reshape(B, S, GDN_V_HEADS, GDN_HEAD_DIM).astype(jnp.float32))
    o = rmsnorm(o, norm_w) * zg
    return o.reshape(B, S, GDN_V_WIDTH).astype(h.dtype) @ w_out


def forgetting_attention(q, k, v, cum):
    B, S, H, DH = q.shape
    NB = S // Q_BLOCK
    cum_k = jnp.moveaxis(cum, 1, 2)
    q_blocks = jnp.moveaxis(q.reshape(B, NB, Q_BLOCK, H, DH), 1, 0)
    c_blocks = jnp.moveaxis(cum_k.reshape(B, H, NB, Q_BLOCK), 2, 0)
    key_pos = jnp.arange(S)

    def block(inp):
        q_b, c_b, start = inp
        logits = jnp.einsum('bqhd,bkhd->bhqk', q_b, k)
        logits = logits + (c_b[..., :, None] - cum_k[..., None, :])
        q_pos = start + jnp.arange(Q_BLOCK)
        mask = key_pos[None, :] <= q_pos[:, None]
        p = jax.nn.softmax(jnp.where(mask, logits, -jnp.inf), axis=-1)
        return jnp.einsum('bhqk,bkhd->bqhd', p, v)

    o = lax.map(block, (q_blocks, c_blocks, jnp.arange(NB) * Q_BLOCK))
    return jnp.moveaxis(o, 0, 1).reshape(B, S, H, DH)


def fox_mixer(h, w_in, f_bias, qn_w, kn_w, w_out):
    B, S, _ = h.shape
    proj = h @ w_in
    q, k, v, z, f = jnp.split(proj, [FOX_WIDTH, 2 * FOX_WIDTH, 3 * FOX_WIDTH, 4 * FOX_WIDTH], axis=-1)
    shp = (B, S, FOX_HEADS, FOX_HEAD_DIM)
    q = rmsnorm(q.reshape(shp).astype(jnp.float32), qn_w) * FOX_HEAD_DIM ** -0.5
    k = rmsnorm(k.reshape(shp).astype(jnp.float32), kn_w)
    v = v.reshape(shp).astype(jnp.float32)
    log_f = jax.nn.log_sigmoid(f.astype(jnp.float32) + f_bias.astype(jnp.float32))
    cum = jnp.cumsum(log_f, axis=1)
    o = forgetting_attention(q, k, v, cum).reshape(B, S, FOX_WIDTH)
    o = o * jax.nn.silu(z.astype(jnp.float32))
    return o.astype(h.dtype) @ w_out


def setup_inputs(seed: int = 0) -> dict:
    key = jax.random.key(seed)
    ks = jax.random.split(key, 20)
    nrm = jax.random.normal
    D = D_MODEL
    dt = jnp.exp(jax.random.uniform(ks[7], (N_LAYERS_A, GDN_V_HEADS),
                                    minval=np.log(1e-3), maxval=np.log(1e-1)))
    return {
        "x": nrm(ks[0], (BATCH, SEQ, D), jnp.float32),
        "c": nrm(ks[1], (BATCH, D), jnp.float32),
        "norm_w": 1.0 + 0.1 * nrm(ks[2], (DEPTH, D), jnp.float32),
        "ada_w": 0.5 * D ** -0.5 * nrm(ks[3], (DEPTH, D, 3 * D), jnp.float32),
        "ada_b": 0.02 * nrm(ks[4], (DEPTH, 3 * D), jnp.float32),
        "a_w_in": D ** -0.5 * nrm(ks[5], (N_LAYERS_A, D, GDN_IN_WIDTH), jnp.float32),
        "a_conv_w": CONV_WIDTH ** -0.5 * nrm(ks[6], (N_LAYERS_A, CONV_WIDTH, GDN_CONV_CH), jnp.float32),
        "a_A_log": jnp.log(jax.random.uniform(ks[8], (N_LAYERS_A, GDN_V_HEADS), minval=1.0, maxval=16.0)),
        "a_dt_bias": dt + jnp.log(-jnp.expm1(-dt)),
        "a_norm_w": 1.0 + 0.1 * nrm(ks[9], (N_LAYERS_A, GDN_HEAD_DIM), jnp.float32),
        "a_w_out": GDN_V_WIDTH ** -0.5 * nrm(ks[10], (N_LAYERS_A, GDN_V_WIDTH, D), jnp.float32),
        "b_w_in": D ** -0.5 * nrm(ks[11], (N_LAYERS_B, D, FOX_IN_WIDTH), jnp.float32),
        "b_f_bias": jax.random.uniform(ks[12], (N_LAYERS_B, FOX_HEADS), minval=1.0, maxval=5.0),
        "b_qn_w": 1.0 + 0.1 * nrm(ks[13], (N_LAYERS_B, FOX_HEAD_DIM), jnp.float32),
        "b_kn_w": 1.0 + 0.1 * nrm(ks[14], (N_LAYERS_B, FOX_HEAD_DIM), jnp.float32),
        "b_w_out": FOX_WIDTH ** -0.5 * nrm(ks[15], (N_LAYERS_B, FOX_WIDTH, D), jnp.float32),
        "final_norm_w": 1.0 + 0.1 * nrm(ks[16], (D,), jnp.float32),
    }


def reference(x, c, norm_w, ada_w, ada_b, a_w_in, a_conv_w, a_A_log, a_dt_bias, a_norm_w,
              a_w_out, b_w_in, b_f_bias, b_qn_w, b_kn_w, b_w_out, final_norm_w):
    cond = jax.nn.silu(c)
    for i in range(DEPTH):
        mod = cond @ ada_w[i] + ada_b[i]
        shift, scale, gate = jnp.split(mod[:, None, :], 3, axis=-1)
        h = rmsnorm(x, norm_w[i]) * (1 + scale) + shift
        j = i // N_MIXERS
        if i % N_MIXERS == 0:
            y = gdn_mixer(h, a_w_in[j], a_conv_w[j], a_A_log[j], a_dt_bias[j], a_norm_w[j], a_w_out[j])
        else:
            y = fox_mixer(h, b_w_in[j], b_f_bias[j], b_qn_w[j], b_kn_w[j], b_w_out[j])
        x = x + gate * y
    return rmsnorm(x, final_norm_w)
```

```python
import functools

import jax
import jax.numpy as jnp
from jax import lax
from jax.experimental import pallas as pl
from jax.experimental.pallas import tpu as pltpu

F32 = jnp.float32
BF16 = jnp.bfloat16

D_MODEL = 1024
DEPTH = 4
EPS = 1e-6
CHUNK = 64

GDN_QK_HEADS = 8
GDN_V_HEADS = 16
GDN_HEAD_DIM = 128
GDN_QK_WIDTH = GDN_QK_HEADS * GDN_HEAD_DIM
GDN_V_WIDTH = GDN_V_HEADS * GDN_HEAD_DIM
GDN_CONV_CH = 2 * GDN_QK_WIDTH + GDN_V_WIDTH
CONV_WIDTH = 4

FOX_HEADS = 16
FOX_HEAD_DIM = 64
FOX_WIDTH = FOX_HEADS * FOX_HEAD_DIM

LANES = 128
SUBLANES = 8
VMEM_LIMIT = 48 * 1024 * 1024

TM_PROJ = 256
TC_DELTA = 256
TQ_ATTN = 256
CONV_GROUP = 512


def _sigmoid(x):
    return 1.0 / (1.0 + jnp.exp(-x))


def _silu(x):
    return x * _sigmoid(x)


def _softplus(x):
    return jnp.maximum(x, 0.0) + jnp.log(1.0 + jnp.exp(-jnp.abs(x)))


def _split3(a):
    hi = a.astype(BF16)
    r = a - hi.astype(F32)
    mid = r.astype(BF16)
    lo = (r - mid.astype(F32)).astype(BF16)
    return hi, mid, lo


def _dot(a, b):
    return jnp.dot(a, b, preferred_element_type=F32)


def _dot_nt(a, b):
    return lax.dot_general(a, b, (((1,), (1,)), ((), ())), preferred_element_type=F32)


def _dot_tn(a, b):
    return lax.dot_general(a, b, (((0,), (0,)), ((), ())), preferred_element_type=F32)


def _modulated_norm(x, mod, nw):
    ms = jnp.mean(x * x, axis=-1, keepdims=True)
    y = x * lax.rsqrt(ms + EPS) * nw
    return y * (1.0 + mod[1:2, :]) + mod[0:1, :]


def _params(*sem):
    return pltpu.CompilerParams(dimension_semantics=sem, vmem_limit_bytes=VMEM_LIMIT)


def _resident(shape):
    nd = len(shape)
    return pl.BlockSpec(shape, lambda *_: (0,) * nd, pipeline_mode=pl.Buffered(1))


def _ada_kernel(c_ref, w_ref, b_ref, o_ref):
    cond = _silu(c_ref[...])
    o_ref[0] = _dot(cond, w_ref[0]) + b_ref[0]


def _ada_mod(c, ada_w, ada_b):
    B, D = c.shape
    depth = ada_w.shape[0]
    out = pl.pallas_call(
        _ada_kernel,
        grid=(depth,),
        in_specs=[
            pl.BlockSpec((B, D), lambda i: (0, 0)),
            pl.BlockSpec((1, D, 3 * D), lambda i: (i, 0, 0)),
            pl.BlockSpec((1, 1, 3 * D), lambda i: (i, 0, 0)),
        ],
        out_specs=pl.BlockSpec((1, B, 3 * D), lambda i: (i, 0, 0)),
        out_shape=jax.ShapeDtypeStruct((depth, B, 3 * D), F32),
        compiler_params=_params("arbitrary"),
        name="ada_mod",
    )(c, ada_w, ada_b.reshape(depth, 1, 3 * D))
    return out.reshape(depth, B, 3, D)


def _gdn_in_kernel(x_ref, mod_ref, nw_ref, w_ref, wba_ref, wbat_ref, cw_ref,
                   alog_ref, dtb_ref, alogc_ref, dtbc_ref,
                   q_ref, k_ref, v_ref, z_ref, bg_ref, gt_ref, cbuf_ref, *, tm):
    s = pl.program_id(1)
    tail = SUBLANES

    @pl.when(s == 0)
    def _():
        cbuf_ref[0:tail, :] = jnp.zeros((tail, GDN_CONV_CH), F32)

    h = _modulated_norm(x_ref[0], mod_ref[0, 0], nw_ref[...])
    hb = h.astype(BF16)

    qscale = GDN_HEAD_DIM ** -0.5
    for c in range(GDN_CONV_CH // CONV_GROUP):
        lo = c * CONV_GROUP
        pre = _dot(hb, w_ref[:, lo:lo + CONV_GROUP])
        cbuf_ref[tail:tail + tm, lo:lo + CONV_GROUP] = pre
        acc = cw_ref[CONV_WIDTH - 1:CONV_WIDTH, lo:lo + CONV_GROUP] * pre
        for j in range(CONV_WIDTH - 1):
            off = tail - (CONV_WIDTH - 1) + j
            acc = acc + cw_ref[j:j + 1, lo:lo + CONV_GROUP] * cbuf_ref[off:off + tm, lo:lo + CONV_GROUP]
        y = _silu(acc)
        if lo < 2 * GDN_QK_WIDTH:
            for hh in range(CONV_GROUP // GDN_HEAD_DIM):
                seg = y[:, hh * GDN_HEAD_DIM:(hh + 1) * GDN_HEAD_DIM]
                inv = lax.rsqrt(jnp.sum(seg * seg, axis=-1, keepdims=True) + EPS)
                col = lo + hh * GDN_HEAD_DIM
                if col < GDN_QK_WIDTH:
                    q_ref[0, :, col:col + GDN_HEAD_DIM] = (seg * inv * qscale).astype(BF16)
                else:
                    col -= GDN_QK_WIDTH
                    k_ref[0, :, col:col + GDN_HEAD_DIM] = (seg * inv).astype(BF16)
        else:
            col = lo - 2 * GDN_QK_WIDTH
            v_ref[0, :, col:col + CONV_GROUP] = y.astype(BF16)
    cbuf_ref[0:tail, :] = cbuf_ref[tm:tm + tail, :]

    for c in range(GDN_V_WIDTH // CONV_GROUP):
        lo = c * CONV_GROUP
        z = _dot(hb, w_ref[:, GDN_CONV_CH + lo:GDN_CONV_CH + lo + CONV_GROUP])
        z_ref[0, :, lo:lo + CONV_GROUP] = z.astype(BF16)

    nh = GDN_V_HEADS
    ba = _dot(hb, wba_ref[...])
    beta = _sigmoid(ba[:, 0:nh])
    g = -jnp.exp(alog_ref[...]) * _softplus(ba[:, LANES:LANES + nh] + dtb_ref[...])
    row = lax.broadcasted_iota(jnp.int32, (tm, tm), 0)
    col = lax.broadcasted_iota(jnp.int32, (tm, tm), 1)
    same = (row // CHUNK) == (col // CHUNK)
    tri_lo = jnp.where(same & (col <= row), 1.0, 0.0).astype(BF16)
    tri_up = jnp.where(same & (row <= col), 1.0, 0.0).astype(BF16)
    gc = sum(_dot(tri_lo, p) for p in _split3(g))
    bg_ref[0, :, 0:nh] = beta
    bg_ref[0, :, nh:2 * nh] = gc
    bat = _dot_nt(wbat_ref[...], hb)
    g_t = -jnp.exp(alogc_ref[...]) * _softplus(bat + dtbc_ref[...])
    gt_ref[0] = sum(_dot(p, tri_up) for p in _split3(g_t))


def _gdn_in(x, mod, nw, w_main, w_ba, w_at, conv_w, a_log, dt_bias):
    B, S, D = x.shape
    tm = TM_PROJ
    nh = GDN_V_HEADS
    kern = functools.partial(_gdn_in_kernel, tm=tm)
    tok = lambda width: pl.BlockSpec((1, tm, width), lambda b, s: (b, s, 0))
    return pl.pallas_call(
        kern,
        grid=(B, S // tm),
        in_specs=[
            tok(D),
            pl.BlockSpec((1, 1, 3, D), lambda b, s: (b, 0, 0, 0)),
            _resident((1, D)),
            _resident(w_main.shape),
            _resident(w_ba.shape),
            _resident(w_at.shape),
            _resident(conv_w.shape),
            _resident((1, nh)), _resident((1, nh)), _resident((nh, 1)), _resident((nh, 1)),
        ],
        out_specs=[
            tok(GDN_QK_WIDTH), tok(GDN_QK_WIDTH), tok(GDN_V_WIDTH), tok(GDN_V_WIDTH),
            tok(2 * nh),
            pl.BlockSpec((1, nh, tm), lambda b, s: (b, 0, s)),
        ],
        out_shape=[
            jax.ShapeDtypeStruct((B, S, GDN_QK_WIDTH), BF16),
            jax.ShapeDtypeStruct((B, S, GDN_QK_WIDTH), BF16),
            jax.ShapeDtypeStruct((B, S, GDN_V_WIDTH), BF16),
            jax.ShapeDtypeStruct((B, S, GDN_V_WIDTH), BF16),
            jax.ShapeDtypeStruct((B, S, 2 * nh), F32),
            jax.ShapeDtypeStruct((B, nh, S), F32),
        ],
        scratch_shapes=[pltpu.VMEM((tm + SUBLANES, GDN_CONV_CH), F32)],
        compiler_params=_params("arbitrary", "arbitrary"),
        name="gdn_in",
    )(x, mod.reshape(B, 1, 3, D), nw.reshape(1, D), w_main, w_ba, w_at, conv_w,
      a_log.reshape(1, nh), dt_bias.reshape(1, nh), a_log.reshape(nh, 1), dt_bias.reshape(nh, 1))


def _unit_lower_inverse(L, eye):
    X = -L
    P = eye + X
    n = L.shape[0]
    p = 2
    while p < n:
        Xb = X.astype(BF16)
        X = _dot(Xb, Xb)
        P = P + _dot(P.astype(BF16), X.astype(BF16))
        p *= 2
    return P


def _delta_kernel(q_ref, k_ref, v_ref, z_ref, bg_ref, gt_ref, nw_ref, o_ref, s_ref, *, tc):
    hq = pl.program_id(1)
    s = pl.program_id(2)
    C = CHUNK
    dh = GDN_HEAD_DIM
    rep = GDN_V_HEADS // GDN_QK_HEADS

    @pl.when(s == 0)
    def _():
        s_ref[...] = jnp.zeros(s_ref.shape, F32)

    row = lax.broadcasted_iota(jnp.int32, (C, C), 0)
    col = lax.broadcasted_iota(jnp.int32, (C, C), 1)
    lower = row >= col
    strict = row > col
    eye = jnp.where(row == col, 1.0, 0.0).astype(F32)
    lane = lax.broadcasted_iota(jnp.int32, (C, 2 * GDN_V_HEADS), 1)
    nw = nw_ref[...]

    state = [s_ref[e] for e in range(rep)]
    g_rows = [gt_ref[0, pl.ds(rep * hq + e, 1), :] for e in range(rep)]
    for c in range(tc // C):
        r0 = c * C
        qb = q_ref[0, r0:r0 + C, :]
        kb = k_ref[0, r0:r0 + C, :]
        qf = qb.astype(F32)
        kf = kb.astype(F32)
        kk = _dot_nt(kb, kb)
        qk = _dot_nt(qb, kb)
        bg = bg_ref[0, r0:r0 + C, :]
        for e in range(rep):
            h = rep * hq + e
            beta = jnp.sum(jnp.where(lane == h, bg, 0.0), axis=-1, keepdims=True)
            gc = jnp.sum(jnp.where(lane == GDN_V_HEADS + h, bg, 0.0), axis=-1, keepdims=True)
            gr = g_rows[e][:, r0:r0 + C]
            decay = jnp.exp(jnp.where(lower, gc - gr, -jnp.inf))
            L = jnp.where(strict, kk * decay, 0.0) * beta
            T = _unit_lower_inverse(L, eye)
            egc = jnp.exp(gc)
            g_last = gc[C - 1:C, :]
            vf = v_ref[0, r0:r0 + C, e * dh:(e + 1) * dh].astype(F32)
            rhs = jnp.concatenate([vf * beta, kf * (beta * egc)], axis=1).astype(BF16)
            uw = _dot(T.astype(BF16), rhs)
            u = uw[:, :dh]
            w = uw[:, dh:]
            S = state[e]
            wq = jnp.concatenate([w, qf * egc], axis=0).astype(BF16)
            r = _dot(wq, S.astype(BF16))
            v_new = u - r[:C]
            vnb = v_new.astype(BF16)
            attn = jnp.where(lower, qk * decay, 0.0).astype(BF16)
            o = r[C:] + _dot(attn, vnb)
            k_dec = (kf * jnp.exp(g_last - gc)).astype(BF16)
            state[e] = S * jnp.exp(g_last) + _dot_tn(k_dec, vnb)
            var = jnp.mean(o * o, axis=-1, keepdims=True)
            zf = z_ref[0, r0:r0 + C, e * dh:(e + 1) * dh].astype(F32)
            o_ref[0, r0:r0 + C, e * dh:(e + 1) * dh] = (o * lax.rsqrt(var + EPS) * nw * _silu(zf)).astype(BF16)
    for e in range(rep):
        s_ref[e] = state[e]


def _gdn_delta(q, k, v, z, bg, gt, nw):
    B, S, _ = q.shape
    tc = TC_DELTA
    rep = GDN_V_HEADS // GDN_QK_HEADS
    dh = GDN_HEAD_DIM
    kern = functools.partial(_delta_kernel, tc=tc)
    return pl.pallas_call(
        kern,
        grid=(B, GDN_QK_HEADS, S // tc),
        in_specs=[
            pl.BlockSpec((1, tc, dh), lambda b, h, s: (b, s, h)),
            pl.BlockSpec((1, tc, dh), lambda b, h, s: (b, s, h)),
            pl.BlockSpec((1, tc, rep * dh), lambda b, h, s: (b, s, h)),
            pl.BlockSpec((1, tc, rep * dh), lambda b, h, s: (b, s, h)),
            pl.BlockSpec((1, tc, 2 * GDN_V_HEADS), lambda b, h, s: (b, s, 0)),
            pl.BlockSpec((1, GDN_V_HEADS, tc), lambda b, h, s: (b, 0, s)),
            pl.BlockSpec((1, dh), lambda b, h, s: (0, 0)),
        ],
        out_specs=pl.BlockSpec((1, tc, rep * dh), lambda b, h, s: (b, s, h)),
        out_shape=jax.ShapeDtypeStruct((B, S, GDN_V_WIDTH), BF16),
        scratch_shapes=[pltpu.VMEM((rep, dh, dh), F32)],
        compiler_params=_params("arbitrary", "arbitrary", "arbitrary"),
        name="gdn_delta",
    )(q, k, v, z, bg, gt, nw.reshape(1, dh))


def _fox_in_kernel(x_ref, mod_ref, nw_ref, w_ref, wft_ref, fb_ref, qkw_ref,
                   q_ref, k_ref, v_ref, z_ref, cum_ref, carry_ref, *, tm):
    s = pl.program_id(1)
    W = FOX_WIDTH
    dh = FOX_HEAD_DIM
    nseg = 2 * FOX_HEADS

    @pl.when(s == 0)
    def _():
        carry_ref[...] = jnp.zeros(carry_ref.shape, F32)

    h = _modulated_norm(x_ref[0], mod_ref[0, 0], nw_ref[...])
    hb = h.astype(BF16)

    qk = _dot(hb, w_ref[:, 0:2 * W])
    seg_r = lax.broadcasted_iota(jnp.int32, (2 * W, LANES), 0) // dh
    seg_c = lax.broadcasted_iota(jnp.int32, (2 * W, LANES), 1)
    seg_sum = jnp.where(seg_r == seg_c, 1.0, 0.0).astype(BF16)
    exp_r = lax.broadcasted_iota(jnp.int32, (LANES, 2 * W), 0)
    exp_c = lax.broadcasted_iota(jnp.int32, (LANES, 2 * W), 1) // dh
    seg_exp = jnp.where(exp_r == exp_c, 1.0, 0.0).astype(BF16)
    ssq = sum(_dot(p, seg_sum) for p in _split3(qk * qk))
    inv = lax.rsqrt(ssq * (1.0 / dh) + EPS)
    scale = sum(_dot(p, seg_exp) for p in _split3(inv))
    qkn = qk * scale * qkw_ref[...]
    vv = _dot(hb, w_ref[:, 2 * W:3 * W])
    for hh in range(FOX_HEADS):
        q_ref[0, hh] = qkn[:, hh * dh:(hh + 1) * dh].astype(BF16)
        k_ref[0, hh] = qkn[:, W + hh * dh:W + (hh + 1) * dh].astype(BF16)
        v_ref[0, hh] = vv[:, hh * dh:(hh + 1) * dh].astype(BF16)
    z_ref[0] = _dot(hb, w_ref[:, 3 * W:4 * W]).astype(BF16)

    xf = _dot_nt(wft_ref[...], hb) + fb_ref[...]
    log_f = -_softplus(-xf)
    row = lax.broadcasted_iota(jnp.int32, (tm, tm), 0)
    col = lax.broadcasted_iota(jnp.int32, (tm, tm), 1)
    tri_up = jnp.where(row <= col, 1.0, 0.0).astype(BF16)
    cum = sum(_dot(p, tri_up) for p in _split3(log_f)) + carry_ref[...]
    cum_ref[0] = cum
    carry_ref[...] = cum[:, tm - 1:tm]


def _fox_in(x, mod, nw, w_main, w_ft, f_bias, qkw):
    B, S, D = x.shape
    tm = TM_PROJ
    H = FOX_HEADS
    dh = FOX_HEAD_DIM
    kern = functools.partial(_fox_in_kernel, tm=tm)
    head_major = pl.BlockSpec((1, H, tm, dh), lambda b, s: (b, 0, s, 0))
    hm_shape = jax.ShapeDtypeStruct((B, H, S, dh), BF16)
    return pl.pallas_call(
        kern,
        grid=(B, S // tm),
        in_specs=[
            pl.BlockSpec((1, tm, D), lambda b, s: (b, s, 0)),
            pl.BlockSpec((1, 1, 3, D), lambda b, s: (b, 0, 0, 0)),
            _resident((1, D)),
            _resident(w_main.shape),
            _resident(w_ft.shape),
            _resident((H, 1)),
            _resident((1, 2 * FOX_WIDTH)),
        ],
        out_specs=[
            head_major, head_major, head_major,
            pl.BlockSpec((1, tm, FOX_WIDTH), lambda b, s: (b, s, 0)),
            pl.BlockSpec((1, H, tm), lambda b, s: (b, 0, s)),
        ],
        out_shape=[
            hm_shape, hm_shape, hm_shape,
            jax.ShapeDtypeStruct((B, S, FOX_WIDTH), BF16),
            jax.ShapeDtypeStruct((B, H, S), F32),
        ],
        scratch_shapes=[pltpu.VMEM((H, 1), F32)],
        compiler_params=_params("arbitrary", "arbitrary"),
        name="fox_in",
    )(x, mod.reshape(B, 1, 3, D), nw.reshape(1, D), w_main, w_ft, f_bias.reshape(H, 1), qkw)


def _fox_attn_kernel(q_ref, k_ref, v_ref, z_ref, cum_ref, o_ref, crow_ref, *, tq, seq):
    hp = pl.program_id(1)
    dh = FOX_HEAD_DIM
    pair = LANES // dh
    row = lax.broadcasted_iota(jnp.int32, (tq, tq), 0)
    col = lax.broadcasted_iota(jnp.int32, (tq, tq), 1)
    causal = row >= col

    for e in range(pair):
        crow_ref[...] = cum_ref[0, pl.ds(pair * hp + e, 1), :]

        def q_block(qi, _):
            q0 = pl.multiple_of(qi * tq, tq)
            q = q_ref[0, e, pl.ds(q0, tq), :]
            cq = crow_ref[:, pl.ds(q0, tq)]
            c0 = cq[:, 0:1]

            def kv_step(k0, bias, carry, masked):
                m, l, acc = carry
                kt = k_ref[0, e, pl.ds(k0, tq), :]
                vt = v_ref[0, e, pl.ds(k0, tq), :]
                sc = _dot_nt(q, kt) + bias
                if masked:
                    sc = jnp.where(causal, sc, -jnp.inf)
                m_new = jnp.maximum(m, jnp.max(sc, axis=-1, keepdims=True))
                alpha = jnp.exp(m - m_new)
                p = jnp.exp(sc - m_new)
                l = alpha * l + jnp.sum(p, axis=-1, keepdims=True)
                acc = alpha * acc + _dot(p.astype(BF16), vt)
                return m_new, l, acc

            def full_step(kj, carry):
                k0 = pl.multiple_of(kj * tq, tq)
                bias = c0 - crow_ref[:, pl.ds(k0, tq)]
                return kv_step(k0, bias, carry, False)

            init = (jnp.full((tq, 1), -jnp.inf, F32), jnp.zeros((tq, 1), F32), jnp.zeros((tq, dh), F32))
            carry = lax.fori_loop(0, qi, full_step, init)
            m, l, acc = kv_step(q0, c0 - cq, carry, True)
            zf = z_ref[0, pl.ds(q0, tq), e * dh:(e + 1) * dh].astype(F32)
            o_ref[0, pl.ds(q0, tq), e * dh:(e + 1) * dh] = (acc / l * _silu(zf)).astype(BF16)
            return 0

        lax.fori_loop(0, seq // tq, q_block, 0)


def _fox_attn(q, k, v, z, cum):
    B, H, S, dh = q.shape
    pair = LANES // dh
    kern = functools.partial(_fox_attn_kernel, tq=TQ_ATTN, seq=S)
    hm = pl.BlockSpec((1, pair, S, dh), lambda b, h: (b, h, 0, 0))
    return pl.pallas_call(
        kern,
        grid=(B, H // pair),
        in_specs=[
            hm, hm, hm,
            pl.BlockSpec((1, S, LANES), lambda b, h: (b, 0, h)),
            pl.BlockSpec((1, H, S), lambda b, h: (b, 0, 0)),
        ],
        out_specs=pl.BlockSpec((1, S, LANES), lambda b, h: (b, 0, h)),
        out_shape=jax.ShapeDtypeStruct((B, S, H * dh), BF16),
        scratch_shapes=[pltpu.VMEM((1, S), F32)],
        compiler_params=_params("arbitrary", "arbitrary"),
        name="fox_attn",
    )(q, k, v, z, cum)


def _out_kernel(o_ref, x_ref, mod_ref, w_ref, fnw_ref, y_ref, *, final):
    y = _dot(o_ref[0], w_ref[...])
    xn = x_ref[0] + mod_ref[0, 0][2:3, :] * y
    if final:
        ms = jnp.mean(xn * xn, axis=-1, keepdims=True)
        xn = xn * lax.rsqrt(ms + EPS) * fnw_ref[...]
    y_ref[0] = xn


def _out_proj(o, x, mod, w_out, fnw, final):
    B, S, D = x.shape
    width = o.shape[-1]
    tm = TM_PROJ
    kern = functools.partial(_out_kernel, final=final)
    return pl.pallas_call(
        kern,
        grid=(B, S // tm),
        in_specs=[
            pl.BlockSpec((1, tm, width), lambda b, s: (b, s, 0)),
            pl.BlockSpec((1, tm, D), lambda b, s: (b, s, 0)),
            pl.BlockSpec((1, 1, 3, D), lambda b, s: (b, 0, 0, 0)),
            _resident(w_out.shape),
            _resident((1, D)),
        ],
        out_specs=pl.BlockSpec((1, tm, D), lambda b, s: (b, s, 0)),
        out_shape=jax.ShapeDtypeStruct((B, S, D), F32),
        compiler_params=_params("arbitrary", "arbitrary"),
        name="out_proj",
    )(o, x, mod.reshape(B, 1, 3, D), w_out, fnw.reshape(1, D))


def kernel(x, c, norm_w, ada_w, ada_b, a_w_in, a_conv_w, a_A_log, a_dt_bias, a_norm_w, a_w_out,
           b_w_in, b_f_bias, b_qn_w, b_kn_w, b_w_out, final_norm_w):
    B, S, D = x.shape
    assert D == D_MODEL and S % max(TM_PROJ, TC_DELTA, TQ_ATTN) == 0
    mods = _ada_mod(c, ada_w, ada_b)
    nh = GDN_V_HEADS
    for i in range(DEPTH):
        j = i // 2
        final = i == DEPTH - 1
        if i % 2 == 0:
            w_in = a_w_in[j]
            w_main = w_in[:, :GDN_CONV_CH + GDN_V_WIDTH].astype(BF16)
            w_b = w_in[:, GDN_CONV_CH + GDN_V_WIDTH:GDN_CONV_CH + GDN_V_WIDTH + nh]
            w_a = w_in[:, GDN_CONV_CH + GDN_V_WIDTH + nh:]
            pad = jnp.zeros((D, LANES - nh), F32)
            w_ba = jnp.concatenate([w_b, pad, w_a, pad], axis=1).astype(BF16)
            w_at = w_a.T.astype(BF16)
            q, k, v, z, bg, gt = _gdn_in(x, mods[i], norm_w[i], w_main, w_ba, w_at, a_conv_w[j],
                                          a_A_log[j], a_dt_bias[j])
            o = _gdn_delta(q, k, v, z, bg, gt, a_norm_w[j])
            w_out = a_w_out[j].astype(BF16)
        else:
            w_in = b_w_in[j]
            w_main = w_in[:, :4 * FOX_WIDTH].astype(BF16)
            w_ft = w_in[:, 4 * FOX_WIDTH:].T.astype(BF16)
            qkw = jnp.concatenate([jnp.tile(b_qn_w[j], FOX_HEADS) * FOX_HEAD_DIM ** -0.5,
                                   jnp.tile(b_kn_w[j], FOX_HEADS)]).reshape(1, 2 * FOX_WIDTH)
            q, k, v, z, cum = _fox_in(x, mods[i], norm_w[i], w_main, w_ft, b_f_bias[j], qkw)
            o = _fox_attn(q, k, v, z, cum)
            w_out = b_w_out[j].astype(BF16)
        x = _out_proj(o, x, mods[i], w_out, final_norm_w, final)
    return x
```

```python
import functools

import jax
import jax.numpy as jnp
from jax import lax
from jax.experimental import pallas as pl
from jax.experimental.pallas import tpu as pltpu

F32 = jnp.float32
BF16 = jnp.bfloat16

D_MODEL = 1024
DEPTH = 4
EPS = 1e-6
CHUNK = 64

GDN_QK_HEADS = 8
GDN_V_HEADS = 16
GDN_HEAD_DIM = 128
GDN_QK_WIDTH = GDN_QK_HEADS * GDN_HEAD_DIM
GDN_V_WIDTH = GDN_V_HEADS * GDN_HEAD_DIM
GDN_CONV_CH = 2 * GDN_QK_WIDTH + GDN_V_WIDTH
CONV_WIDTH = 4

FOX_HEADS = 16
FOX_HEAD_DIM = 64
FOX_WIDTH = FOX_HEADS * FOX_HEAD_DIM

LANES = 128
SUBLANES = 8
BF16_ROWS = 16
LOG2E = 1.4426950408889634
VMEM_LIMIT = 48 * 1024 * 1024

TM_PROJ = 256
TC_DELTA = 256
TQ_ATTN = 256
CONV_GROUP = 512
FOX_GROUP = 512
KT_BLOCK = 512


def _sigmoid(x):
    return 1.0 / (1.0 + jnp.exp(-x))


def _silu(x):
    return x * _sigmoid(x)


def _softplus(x):
    return jnp.maximum(x, 0.0) + jnp.log(1.0 + jnp.exp(-jnp.abs(x)))


def _split3(a):
    hi = a.astype(BF16)
    r = a - hi.astype(F32)
    mid = r.astype(BF16)
    lo = (r - mid.astype(F32)).astype(BF16)
    return hi, mid, lo


def _dot(a, b):
    return jnp.dot(a, b, preferred_element_type=F32)


def _dot_nt(a, b):
    return lax.dot_general(a, b, (((1,), (1,)), ((), ())), preferred_element_type=F32)


def _dot_tn(a, b):
    return lax.dot_general(a, b, (((0,), (0,)), ((), ())), preferred_element_type=F32)


def _modulated_norm(x, mod, nw):
    ms = jnp.mean(x * x, axis=-1, keepdims=True)
    y = x * lax.rsqrt(ms + EPS) * nw
    return y * (1.0 + mod[1:2, :]) + mod[0:1, :]


def _params(*sem):
    return pltpu.CompilerParams(dimension_semantics=sem, vmem_limit_bytes=VMEM_LIMIT)


def _resident(shape):
    nd = len(shape)
    return pl.BlockSpec(shape, lambda *_: (0,) * nd, pipeline_mode=pl.Buffered(1))


def _ada_kernel(c_ref, w_ref, b_ref, o_ref):
    cond = _silu(c_ref[...])
    o_ref[0] = _dot(cond, w_ref[0]) + b_ref[0]


def _ada_mod(c, ada_w, ada_b):
    B, D = c.shape
    depth = ada_w.shape[0]
    out = pl.pallas_call(
        _ada_kernel,
        grid=(depth,),
        in_specs=[
            pl.BlockSpec((B, D), lambda i: (0, 0)),
            pl.BlockSpec((1, D, 3 * D), lambda i: (i, 0, 0)),
            pl.BlockSpec((1, 1, 3 * D), lambda i: (i, 0, 0)),
        ],
        out_specs=pl.BlockSpec((1, B, 3 * D), lambda i: (i, 0, 0)),
        out_shape=jax.ShapeDtypeStruct((depth, B, 3 * D), F32),
        compiler_params=_params("arbitrary"),
        name="ada_mod",
    )(c, ada_w, ada_b.reshape(depth, 1, 3 * D))
    return out.reshape(depth, B, 3, D)


def _gdn_in_kernel(x_ref, mod_ref, nw_ref, w_ref, wba_ref, wbat_ref, cw_ref,
                   alog_ref, dtb_ref, alogc_ref, dtbc_ref,
                   q_ref, k_ref, v_ref, z_ref, bg_ref, gt_ref, cbuf_ref, *, tm):
    s = pl.program_id(1)
    tail = SUBLANES

    @pl.when(s == 0)
    def _():
        cbuf_ref[0:tail, :] = jnp.zeros((tail, GDN_CONV_CH), F32)

    h = _modulated_norm(x_ref[0], mod_ref[0, 0], nw_ref[...])
    hb = h.astype(BF16)

    qscale = GDN_HEAD_DIM ** -0.5
    for c in range(GDN_CONV_CH // CONV_GROUP):
        lo = c * CONV_GROUP
        pre = _dot(hb, w_ref[:, lo:lo + CONV_GROUP])
        cbuf_ref[tail:tail + tm, lo:lo + CONV_GROUP] = pre
        acc = cw_ref[CONV_WIDTH - 1:CONV_WIDTH, lo:lo + CONV_GROUP] * pre
        for j in range(CONV_WIDTH - 1):
            off = tail - (CONV_WIDTH - 1) + j
            acc = acc + cw_ref[j:j + 1, lo:lo + CONV_GROUP] * cbuf_ref[off:off + tm, lo:lo + CONV_GROUP]
        y = _silu(acc)
        if lo < 2 * GDN_QK_WIDTH:
            for hh in range(CONV_GROUP // GDN_HEAD_DIM):
                seg = y[:, hh * GDN_HEAD_DIM:(hh + 1) * GDN_HEAD_DIM]
                inv = lax.rsqrt(jnp.sum(seg * seg, axis=-1, keepdims=True) + EPS)
                col = lo + hh * GDN_HEAD_DIM
                if col < GDN_QK_WIDTH:
                    q_ref[0, :, col:col + GDN_HEAD_DIM] = (seg * inv * qscale).astype(BF16)
                else:
                    col -= GDN_QK_WIDTH
                    k_ref[0, :, col:col + GDN_HEAD_DIM] = (seg * inv).astype(BF16)
        else:
            col = lo - 2 * GDN_QK_WIDTH
            v_ref[0, :, col:col + CONV_GROUP] = y.astype(BF16)
    cbuf_ref[0:tail, :] = cbuf_ref[tm:tm + tail, :]

    for c in range(GDN_V_WIDTH // CONV_GROUP):
        lo = c * CONV_GROUP
        z = _dot(hb, w_ref[:, GDN_CONV_CH + lo:GDN_CONV_CH + lo + CONV_GROUP])
        z_ref[0, :, lo:lo + CONV_GROUP] = z.astype(BF16)

    nh = GDN_V_HEADS
    ba = _dot(hb, wba_ref[...])
    beta = _sigmoid(ba[:, 0:nh])
    g = -jnp.exp(alog_ref[...]) * _softplus(ba[:, LANES:LANES + nh] + dtb_ref[...])
    row = lax.broadcasted_iota(jnp.int32, (tm, tm), 0)
    col = lax.broadcasted_iota(jnp.int32, (tm, tm), 1)
    same = (row // CHUNK) == (col // CHUNK)
    tri_lo = jnp.where(same & (col <= row), 1.0, 0.0).astype(BF16)
    tri_up = jnp.where(same & (row <= col), 1.0, 0.0).astype(BF16)
    gc = sum(_dot(tri_lo, p) for p in _split3(g))
    bg_ref[0, :, 0:nh] = beta
    bg_ref[0, :, nh:2 * nh] = gc
    bat = _dot_nt(wbat_ref[...], hb)
    g_t = -jnp.exp(alogc_ref[...]) * _softplus(bat + dtbc_ref[...])
    gt_ref[0] = sum(_dot(p, tri_up) for p in _split3(g_t))


def _gdn_in(x, mod, nw, w_main, w_ba, w_at, conv_w, a_log, dt_bias):
    B, S, D = x.shape
    tm = TM_PROJ
    nh = GDN_V_HEADS
    kern = functools.partial(_gdn_in_kernel, tm=tm)
    tok = lambda width: pl.BlockSpec((1, tm, width), lambda b, s: (b, s, 0))
    return pl.pallas_call(
        kern,
        grid=(B, S // tm),
        in_specs=[
            tok(D),
            pl.BlockSpec((1, 1, 3, D), lambda b, s: (b, 0, 0, 0)),
            _resident((1, D)),
            _resident(w_main.shape),
            _resident(w_ba.shape),
            _resident(w_at.shape),
            _resident(conv_w.shape),
            _resident((1, nh)), _resident((1, nh)), _resident((nh, 1)), _resident((nh, 1)),
        ],
        out_specs=[
            tok(GDN_QK_WIDTH), tok(GDN_QK_WIDTH), tok(GDN_V_WIDTH), tok(GDN_V_WIDTH),
            tok(2 * nh),
            pl.BlockSpec((1, nh, tm), lambda b, s: (b, 0, s)),
        ],
        out_shape=[
            jax.ShapeDtypeStruct((B, S, GDN_QK_WIDTH), BF16),
            jax.ShapeDtypeStruct((B, S, GDN_QK_WIDTH), BF16),
            jax.ShapeDtypeStruct((B, S, GDN_V_WIDTH), BF16),
            jax.ShapeDtypeStruct((B, S, GDN_V_WIDTH), BF16),
            jax.ShapeDtypeStruct((B, S, 2 * nh), F32),
            jax.ShapeDtypeStruct((B, nh, S), F32),
        ],
        scratch_shapes=[pltpu.VMEM((tm + SUBLANES, GDN_CONV_CH), F32)],
        compiler_params=_params("arbitrary", "arbitrary"),
        name="gdn_in",
    )(x, mod.reshape(B, 1, 3, D), nw.reshape(1, D), w_main, w_ba, w_at, conv_w,
      a_log.reshape(1, nh), dt_bias.reshape(1, nh), a_log.reshape(nh, 1), dt_bias.reshape(nh, 1))


def _unit_lower_inverse(L, eye):
    X = -L
    P = eye + X
    n = L.shape[0]
    p = 2
    while p < n:
        Xb = X.astype(BF16)
        X = _dot(Xb, Xb)
        P = P + _dot(P.astype(BF16), X.astype(BF16))
        p *= 2
    return P


def _delta_kernel(q_ref, k_ref, v_ref, z_ref, bg_ref, gt_ref, nw_ref, o_ref, s_ref, *, tc):
    hq = pl.program_id(1)
    s = pl.program_id(2)
    C = CHUNK
    dh = GDN_HEAD_DIM
    rep = GDN_V_HEADS // GDN_QK_HEADS

    @pl.when(s == 0)
    def _():
        s_ref[...] = jnp.zeros(s_ref.shape, F32)

    row = lax.broadcasted_iota(jnp.int32, (C, C), 0)
    col = lax.broadcasted_iota(jnp.int32, (C, C), 1)
    lower = row >= col
    strict = row > col
    eye = jnp.where(row == col, 1.0, 0.0).astype(F32)
    lane = lax.broadcasted_iota(jnp.int32, (C, 2 * GDN_V_HEADS), 1)
    nw = nw_ref[...]

    state = [s_ref[e] for e in range(rep)]
    g_rows = [gt_ref[0, pl.ds(rep * hq + e, 1), :] for e in range(rep)]
    for c in range(tc // C):
        r0 = c * C
        qb = q_ref[0, r0:r0 + C, :]
        kb = k_ref[0, r0:r0 + C, :]
        qf = qb.astype(F32)
        kf = kb.astype(F32)
        kk = _dot_nt(kb, kb)
        qk = _dot_nt(qb, kb)
        bg = bg_ref[0, r0:r0 + C, :]
        for e in range(rep):
            h = rep * hq + e
            beta = jnp.sum(jnp.where(lane == h, bg, 0.0), axis=-1, keepdims=True)
            gc = jnp.sum(jnp.where(lane == GDN_V_HEADS + h, bg, 0.0), axis=-1, keepdims=True)
            gr = g_rows[e][:, r0:r0 + C]
            decay = jnp.exp(jnp.where(lower, gc - gr, -jnp.inf))
            L = jnp.where(strict, kk * decay, 0.0) * beta
            T = _unit_lower_inverse(L, eye)
            egc = jnp.exp(gc)
            g_last = gc[C - 1:C, :]
            vf = v_ref[0, r0:r0 + C, e * dh:(e + 1) * dh].astype(F32)
            rhs = jnp.concatenate([vf * beta, kf * (beta * egc)], axis=1).astype(BF16)
            uw = _dot(T.astype(BF16), rhs)
            u = uw[:, :dh]
            w = uw[:, dh:]
            S = state[e]
            wq = jnp.concatenate([w, qf * egc], axis=0).astype(BF16)
            r = _dot(wq, S.astype(BF16))
            v_new = u - r[:C]
            vnb = v_new.astype(BF16)
            attn = jnp.where(lower, qk * decay, 0.0).astype(BF16)
            o = r[C:] + _dot(attn, vnb)
            k_dec = (kf * jnp.exp(g_last - gc)).astype(BF16)
            state[e] = S * jnp.exp(g_last) + _dot_tn(k_dec, vnb)
            var = jnp.mean(o * o, axis=-1, keepdims=True)
            zf = z_ref[0, r0:r0 + C, e * dh:(e + 1) * dh].astype(F32)
            o_ref[0, r0:r0 + C, e * dh:(e + 1) * dh] = (o * lax.rsqrt(var + EPS) * nw * _silu(zf)).astype(BF16)
    for e in range(rep):
        s_ref[e] = state[e]


def _gdn_delta(q, k, v, z, bg, gt, nw):
    B, S, _ = q.shape
    tc = TC_DELTA
    rep = GDN_V_HEADS // GDN_QK_HEADS
    dh = GDN_HEAD_DIM
    kern = functools.partial(_delta_kernel, tc=tc)
    return pl.pallas_call(
        kern,
        grid=(B, GDN_QK_HEADS, S // tc),
        in_specs=[
            pl.BlockSpec((1, tc, dh), lambda b, h, s: (b, s, h)),
            pl.BlockSpec((1, tc, dh), lambda b, h, s: (b, s, h)),
            pl.BlockSpec((1, tc, rep * dh), lambda b, h, s: (b, s, h)),
            pl.BlockSpec((1, tc, rep * dh), lambda b, h, s: (b, s, h)),
            pl.BlockSpec((1, tc, 2 * GDN_V_HEADS), lambda b, h, s: (b, s, 0)),
            pl.BlockSpec((1, GDN_V_HEADS, tc), lambda b, h, s: (b, 0, s)),
            pl.BlockSpec((1, dh), lambda b, h, s: (0, 0)),
        ],
        out_specs=pl.BlockSpec((1, tc, rep * dh), lambda b, h, s: (b, s, h)),
        out_shape=jax.ShapeDtypeStruct((B, S, GDN_V_WIDTH), BF16),
        scratch_shapes=[pltpu.VMEM((rep, dh, dh), F32)],
        compiler_params=_params("arbitrary", "arbitrary", "arbitrary"),
        name="gdn_delta",
    )(q, k, v, z, bg, gt, nw.reshape(1, dh))


def _fox_in_kernel(x_ref, mod_ref, nw_ref, wt_ref, wz_ref, wft_ref, fb_ref, qw_ref, kw_ref,
                   qt_ref, kt_ref, vt_ref, z_ref, cum_ref, carry_ref, *, tm):
    s = pl.program_id(1)
    W = FOX_WIDTH
    dh = FOX_HEAD_DIM

    @pl.when(s == 0)
    def _():
        carry_ref[...] = jnp.zeros(carry_ref.shape, F32)

    h = _modulated_norm(x_ref[0], mod_ref[0, 0], nw_ref[...])
    hb = h.astype(BF16)

    grp = FOX_GROUP
    for g in range(3 * W // grp):
        blk = _dot_nt(wt_ref[g * grp:(g + 1) * grp, :], hb)
        r0 = (g * grp) % W
        if g * grp < 2 * W:
            is_q = g * grp < W
            wcol = qw_ref[...] if is_q else kw_ref[...]
            dst = qt_ref if is_q else kt_ref
            for hh in range(grp // dh):
                seg = blk[hh * dh:(hh + 1) * dh]
                inv = lax.rsqrt(jnp.mean(seg * seg, axis=0, keepdims=True) + EPS)
                dst[0, r0 + hh * dh:r0 + (hh + 1) * dh, :] = (seg * inv * wcol).astype(BF16)
        else:
            vt_ref[0, r0:r0 + grp, :] = blk.astype(BF16)
    z_ref[0] = _dot(hb, wz_ref[...]).astype(BF16)

    xf = _dot_nt(wft_ref[...], hb) + fb_ref[...]
    log_f = -_softplus(-xf)
    row = lax.broadcasted_iota(jnp.int32, (tm, tm), 0)
    col = lax.broadcasted_iota(jnp.int32, (tm, tm), 1)
    tri_up = jnp.where(row <= col, 1.0, 0.0).astype(BF16)
    cum = sum(_dot(p, tri_up) for p in _split3(log_f)) + carry_ref[...]
    cum_ref[0] = cum
    carry_ref[...] = cum[:, tm - 1:tm]


def _fox_in(x, mod, nw, w_qkv_t, w_z, w_ft, f_bias, qw, kw):
    B, S, D = x.shape
    tm = TM_PROJ
    H = FOX_HEADS
    dh = FOX_HEAD_DIM
    W = FOX_WIDTH
    kern = functools.partial(_fox_in_kernel, tm=tm)
    feat_major = pl.BlockSpec((1, W, tm), lambda b, s: (b, 0, s))
    fm_shape = jax.ShapeDtypeStruct((B, W, S), BF16)
    return pl.pallas_call(
        kern,
        grid=(B, S // tm),
        in_specs=[
            pl.BlockSpec((1, tm, D), lambda b, s: (b, s, 0)),
            pl.BlockSpec((1, 1, 3, D), lambda b, s: (b, 0, 0, 0)),
            _resident((1, D)),
            _resident(w_qkv_t.shape),
            _resident(w_z.shape),
            _resident(w_ft.shape),
            _resident((H, 1)),
            _resident((dh, 1)),
            _resident((dh, 1)),
        ],
        out_specs=[
            feat_major, feat_major, feat_major,
            pl.BlockSpec((1, tm, W), lambda b, s: (b, s, 0)),
            pl.BlockSpec((1, H, tm), lambda b, s: (b, 0, s)),
        ],
        out_shape=[
            fm_shape, fm_shape, fm_shape,
            jax.ShapeDtypeStruct((B, S, W), BF16),
            jax.ShapeDtypeStruct((B, H, S), F32),
        ],
        scratch_shapes=[pltpu.VMEM((H, 1), F32)],
        compiler_params=_params("arbitrary", "arbitrary"),
        name="fox_in",
    )(x, mod.reshape(B, 1, 3, D), nw.reshape(1, D), w_qkv_t, w_z, w_ft, f_bias.reshape(H, 1),
      qw.reshape(dh, 1), kw.reshape(dh, 1))


def _fox_attn_kernel(qt_ref, kt_ref, vt_ref, z_ref, cum_ref, o_ref, qa_ref, ka_ref, st_ref, p_ref, *, tq, seq):
    hp = pl.program_id(1)
    dh = FOX_HEAD_DIM
    pair = LANES // dh
    nsplit = 3
    kidx = lax.broadcasted_iota(jnp.int32, (tq, tq), 0)
    qidx = lax.broadcasted_iota(jnp.int32, (tq, tq), 1)
    causal = kidx <= qidx

    pick = jnp.where(lax.broadcasted_iota(jnp.int32, (BF16_ROWS, seq), 0) < nsplit, 1.0, 0.0).astype(BF16)
    for e in range(pair):
        qa_ref[e, 0:dh, :] = qt_ref[0, e * dh:(e + 1) * dh, :]
        qa_ref[e, dh:dh + BF16_ROWS, :] = pick
        qa_ref[e, dh + BF16_ROWS:, :] = jnp.zeros((LANES - dh - BF16_ROWS, seq), BF16)
        crow = cum_ref[0, pl.ds(pair * hp + e, 1), :]
        parts = [p.astype(F32) for p in _split3(crow * (-LOG2E))]
        btile = jnp.concatenate(parts + [jnp.zeros((SUBLANES - nsplit, seq), F32)], axis=0)
        for blk in range(seq // KT_BLOCK):
            sl = slice(blk * KT_BLOCK, (blk + 1) * KT_BLOCK)
            top = jnp.concatenate([kt_ref[0, e * dh:(e + 1) * dh, sl].astype(F32), btile[:, sl],
                                   jnp.zeros((LANES - dh - SUBLANES, KT_BLOCK), F32)], axis=0)
            ka_ref[e, sl, :] = top.T.astype(BF16)

    def softmax_step(st, m, l):
        m_new = jnp.maximum(m, jnp.max(st, axis=0, keepdims=True))
        alpha = jnp.exp2(m - m_new)
        p = jnp.exp2(st - m_new)
        return m_new, alpha, alpha * l + jnp.sum(p, axis=0, keepdims=True), p.astype(BF16)

    def q_block(qi, _):
        q0 = pl.multiple_of(qi * tq, tq)
        qas = [qa_ref[e, :, pl.ds(q0, tq)] for e in range(pair)]

        def scores(e, k0):
            return _dot(ka_ref[e, pl.ds(k0, tq), :], qas[e])

        def weighted_values(e, k0, p):
            return _dot(vt_ref[0, e * dh:(e + 1) * dh, pl.ds(k0, tq)], p)

        def body(kj, carry):
            k_next = pl.multiple_of((kj + 1) * tq, tq)
            k_prev = pl.multiple_of(jnp.maximum(kj - 1, 0) * tq, tq)
            new = []
            for e in range(pair):
                m, l, acc, a_prev = carry[e]
                st = st_ref[e]
                pv_prev = weighted_values(e, k_prev, p_ref[e])
                st_ref[e] = scores(e, k_next)
                m, alpha, l, p = softmax_step(st, m, l)
                p_ref[e] = p
                new.append((m, l, a_prev * acc + pv_prev, alpha))
            return tuple(new)

        for e in range(pair):
            st_ref[e] = scores(e, 0)
            p_ref[e] = jnp.zeros((tq, tq), BF16)
        init = tuple((jnp.full((1, tq), -jnp.inf, F32), jnp.zeros((1, tq), F32), jnp.zeros((dh, tq), F32),
                      jnp.ones((1, tq), F32)) for _ in range(pair))
        carry = lax.fori_loop(0, qi, body, init)
        k_prev = pl.multiple_of(jnp.maximum(qi - 1, 0) * tq, tq)
        outs = []
        for e in range(pair):
            m, l, acc, a_prev = carry[e]
            pv_prev = weighted_values(e, k_prev, p_ref[e])
            st = jnp.where(causal, st_ref[e], -jnp.inf)
            m, alpha, l, p = softmax_step(st, m, l)
            acc = alpha * (a_prev * acc + pv_prev) + weighted_values(e, q0, p)
            outs.append(acc * (1.0 / l))
        o = jnp.concatenate(outs, axis=0).T
        zf = z_ref[0, pl.ds(q0, tq), :].astype(F32)
        o_ref[0, pl.ds(q0, tq), :] = (o * _silu(zf)).astype(BF16)
        return 0

    lax.fori_loop(0, seq // tq, q_block, 0)


def _fox_attn(qt, kt, vt, z, cum):
    B, W, S = qt.shape
    H = FOX_HEADS
    pair = LANES // FOX_HEAD_DIM
    kern = functools.partial(_fox_attn_kernel, tq=TQ_ATTN, seq=S)
    fm = pl.BlockSpec((1, LANES, S), lambda b, h: (b, h, 0))
    return pl.pallas_call(
        kern,
        grid=(B, H // pair),
        in_specs=[
            fm, fm, fm,
            pl.BlockSpec((1, S, LANES), lambda b, h: (b, 0, h)),
            pl.BlockSpec((1, H, S), lambda b, h: (b, 0, 0)),
        ],
        out_specs=pl.BlockSpec((1, S, LANES), lambda b, h: (b, 0, h)),
        out_shape=jax.ShapeDtypeStruct((B, S, W), BF16),
        scratch_shapes=[pltpu.VMEM((pair, LANES, S), BF16), pltpu.VMEM((pair, S, LANES), BF16),
                        pltpu.VMEM((pair, TQ_ATTN, TQ_ATTN), F32), pltpu.VMEM((pair, TQ_ATTN, TQ_ATTN), BF16)],
        compiler_params=_params("arbitrary", "arbitrary"),
        name="fox_attn",
    )(qt, kt, vt, z, cum)


def _out_kernel(o_ref, x_ref, mod_ref, w_ref, fnw_ref, y_ref, *, final):
    y = _dot(o_ref[0], w_ref[...])
    xn = x_ref[0] + mod_ref[0, 0][2:3, :] * y
    if final:
        ms = jnp.mean(xn * xn, axis=-1, keepdims=True)
        xn = xn * lax.rsqrt(ms + EPS) * fnw_ref[...]
    y_ref[0] = xn


def _out_proj(o, x, mod, w_out, fnw, final):
    B, S, D = x.shape
    width = o.shape[-1]
    tm = TM_PROJ
    kern = functools.partial(_out_kernel, final=final)
    return pl.pallas_call(
        kern,
        grid=(B, S // tm),
        in_specs=[
            pl.BlockSpec((1, tm, width), lambda b, s: (b, s, 0)),
            pl.BlockSpec((1, tm, D), lambda b, s: (b, s, 0)),
            pl.BlockSpec((1, 1, 3, D), lambda b, s: (b, 0, 0, 0)),
            _resident(w_out.shape),
            _resident((1, D)),
        ],
        out_specs=pl.BlockSpec((1, tm, D), lambda b, s: (b, s, 0)),
        out_shape=jax.ShapeDtypeStruct((B, S, D), F32),
        compiler_params=_params("arbitrary", "arbitrary"),
        name="out_proj",
    )(o, x, mod.reshape(B, 1, 3, D), w_out, fnw.reshape(1, D))


def kernel(x, c, norm_w, ada_w, ada_b, a_w_in, a_conv_w, a_A_log, a_dt_bias, a_norm_w, a_w_out,
           b_w_in, b_f_bias, b_qn_w, b_kn_w, b_w_out, final_norm_w):
    B, S, D = x.shape
    assert D == D_MODEL and S % max(TM_PROJ, TC_DELTA, TQ_ATTN) == 0
    mods = _ada_mod(c, ada_w, ada_b)
    nh = GDN_V_HEADS
    for i in range(DEPTH):
        j = i // 2
        final = i == DEPTH - 1
        if i % 2 == 0:
            w_in = a_w_in[j]
            w_main = w_in[:, :GDN_CONV_CH + GDN_V_WIDTH].astype(BF16)
            w_b = w_in[:, GDN_CONV_CH + GDN_V_WIDTH:GDN_CONV_CH + GDN_V_WIDTH + nh]
            w_a = w_in[:, GDN_CONV_CH + GDN_V_WIDTH + nh:]
            pad = jnp.zeros((D, LANES - nh), F32)
            w_ba = jnp.concatenate([w_b, pad, w_a, pad], axis=1).astype(BF16)
            w_at = w_a.T.astype(BF16)
            q, k, v, z, bg, gt = _gdn_in(x, mods[i], norm_w[i], w_main, w_ba, w_at, a_conv_w[j],
                                          a_A_log[j], a_dt_bias[j])
            o = _gdn_delta(q, k, v, z, bg, gt, a_norm_w[j])
            w_out = a_w_out[j].astype(BF16)
        else:
            w_in = b_w_in[j]
            w_qkv_t = w_in[:, :3 * FOX_WIDTH].T.astype(BF16)
            w_z = w_in[:, 3 * FOX_WIDTH:4 * FOX_WIDTH].astype(BF16)
            w_ft = w_in[:, 4 * FOX_WIDTH:].T.astype(BF16)
            qw = b_qn_w[j] * (FOX_HEAD_DIM ** -0.5 * LOG2E)
            qt, kt, vt, z, cum = _fox_in(x, mods[i], norm_w[i], w_qkv_t, w_z, w_ft, b_f_bias[j], qw, b_kn_w[j])
            o = _fox_attn(qt, kt, vt, z, cum)
            w_out = b_w_out[j].astype(BF16)
        x = _out_proj(o, x, mods[i], w_out, final_norm_w, final)
    return x
```

```python
import functools

import jax
import jax.numpy as jnp
from jax import lax
from jax.experimental import pallas as pl
from jax.experimental.pallas import tpu as pltpu

F32 = jnp.float32
BF16 = jnp.bfloat16

D_MODEL = 1024
DEPTH = 4
EPS = 1e-6
CHUNK = 64

GDN_QK_HEADS = 8
GDN_V_HEADS = 16
GDN_HEAD_DIM = 128
GDN_QK_WIDTH = GDN_QK_HEADS * GDN_HEAD_DIM
GDN_V_WIDTH = GDN_V_HEADS * GDN_HEAD_DIM
GDN_CONV_CH = 2 * GDN_QK_WIDTH + GDN_V_WIDTH
CONV_WIDTH = 4

FOX_HEADS = 16
FOX_HEAD_DIM = 64
FOX_WIDTH = FOX_HEADS * FOX_HEAD_DIM

LANES = 128
SUBLANES = 8
BF16_ROWS = 16
LOG2E = 1.4426950408889634
VMEM_LIMIT = 48 * 1024 * 1024

TM_PROJ = 256
TC_DELTA = 256
TQ_ATTN = 256
CONV_GROUP = 512
FOX_GROUP = 512
KT_BLOCK = 512
ATTN_UNROLL = 8
ST_RING = 4
P_RING = 2


def _sigmoid(x):
    return 1.0 / (1.0 + jnp.exp(-x))


def _silu(x):
    return x * _sigmoid(x)


def _softplus(x):
    return jnp.maximum(x, 0.0) + jnp.log(1.0 + jnp.exp(-jnp.abs(x)))


def _split3(a):
    hi = a.astype(BF16)
    r = a - hi.astype(F32)
    mid = r.astype(BF16)
    lo = (r - mid.astype(F32)).astype(BF16)
    return hi, mid, lo


def _dot(a, b):
    return jnp.dot(a, b, preferred_element_type=F32)


def _dot_nt(a, b):
    return lax.dot_general(a, b, (((1,), (1,)), ((), ())), preferred_element_type=F32)


def _dot_tn(a, b):
    return lax.dot_general(a, b, (((0,), (0,)), ((), ())), preferred_element_type=F32)


def _modulated_norm(x, mod, nw):
    ms = jnp.mean(x * x, axis=-1, keepdims=True)
    y = x * lax.rsqrt(ms + EPS) * nw
    return y * (1.0 + mod[1:2, :]) + mod[0:1, :]


def _params(*sem):
    return pltpu.CompilerParams(dimension_semantics=sem, vmem_limit_bytes=VMEM_LIMIT)


def _resident(shape):
    nd = len(shape)
    return pl.BlockSpec(shape, lambda *_: (0,) * nd, pipeline_mode=pl.Buffered(1))


def _ada_kernel(c_ref, w_ref, b_ref, o_ref):
    cond = _silu(c_ref[...])
    o_ref[0] = _dot(cond, w_ref[0]) + b_ref[0]


def _ada_mod(c, ada_w, ada_b):
    B, D = c.shape
    depth = ada_w.shape[0]
    out = pl.pallas_call(
        _ada_kernel,
        grid=(depth,),
        in_specs=[
            pl.BlockSpec((B, D), lambda i: (0, 0)),
            pl.BlockSpec((1, D, 3 * D), lambda i: (i, 0, 0)),
            pl.BlockSpec((1, 1, 3 * D), lambda i: (i, 0, 0)),
        ],
        out_specs=pl.BlockSpec((1, B, 3 * D), lambda i: (i, 0, 0)),
        out_shape=jax.ShapeDtypeStruct((depth, B, 3 * D), F32),
        compiler_params=_params("arbitrary"),
        name="ada_mod",
    )(c, ada_w, ada_b.reshape(depth, 1, 3 * D))
    return out.reshape(depth, B, 3, D)


def _gdn_in_kernel(x_ref, mod_ref, nw_ref, w_ref, wba_ref, wbat_ref, cw_ref,
                   alog_ref, dtb_ref, alogc_ref, dtbc_ref,
                   q_ref, k_ref, kt_ref, v_ref, z_ref, bg_ref, gt_ref, cbuf_ref, *, tm):
    s = pl.program_id(1)
    tail = SUBLANES
    nchunk = tm // CHUNK

    @pl.when(s == 0)
    def _():
        cbuf_ref[0:tail, :] = jnp.zeros((tail, GDN_CONV_CH), F32)

    h = _modulated_norm(x_ref[0], mod_ref[0, 0], nw_ref[...])
    hb = h.astype(BF16)

    qscale = GDN_HEAD_DIM ** -0.5
    for c in range(GDN_CONV_CH // CONV_GROUP):
        lo = c * CONV_GROUP
        pre = _dot(hb, w_ref[:, lo:lo + CONV_GROUP])
        cbuf_ref[tail:tail + tm, lo:lo + CONV_GROUP] = pre
        acc = cw_ref[CONV_WIDTH - 1:CONV_WIDTH, lo:lo + CONV_GROUP] * pre
        for j in range(CONV_WIDTH - 1):
            off = tail - (CONV_WIDTH - 1) + j
            acc = acc + cw_ref[j:j + 1, lo:lo + CONV_GROUP] * cbuf_ref[off:off + tm, lo:lo + CONV_GROUP]
        y = _silu(acc)
        if lo < 2 * GDN_QK_WIDTH:
            for hh in range(CONV_GROUP // GDN_HEAD_DIM):
                seg = y[:, hh * GDN_HEAD_DIM:(hh + 1) * GDN_HEAD_DIM]
                inv = lax.rsqrt(jnp.sum(seg * seg, axis=-1, keepdims=True) + EPS)
                col = lo + hh * GDN_HEAD_DIM
                if col < GDN_QK_WIDTH:
                    q_ref[0, :, col:col + GDN_HEAD_DIM] = (seg * inv * qscale).astype(BF16)
                else:
                    col -= GDN_QK_WIDTH
                    kn = seg * inv
                    k_ref[0, :, col:col + GDN_HEAD_DIM] = kn.astype(BF16)
                    knt = kn.T.astype(BF16)
                    for cc in range(nchunk):
                        kt_ref[0, cc, col:col + GDN_HEAD_DIM, :] = knt[:, cc * CHUNK:(cc + 1) * CHUNK]
        else:
            col = lo - 2 * GDN_QK_WIDTH
            v_ref[0, :, col:col + CONV_GROUP] = y.astype(BF16)
    cbuf_ref[0:tail, :] = cbuf_ref[tm:tm + tail, :]

    for c in range(GDN_V_WIDTH // CONV_GROUP):
        lo = c * CONV_GROUP
        z = _dot(hb, w_ref[:, GDN_CONV_CH + lo:GDN_CONV_CH + lo + CONV_GROUP])
        z_ref[0, :, lo:lo + CONV_GROUP] = z.astype(BF16)

    nh = GDN_V_HEADS
    ba = _dot(hb, wba_ref[...])
    beta = _sigmoid(ba[:, 0:nh])
    g = -jnp.exp(alog_ref[...]) * _softplus(ba[:, LANES:LANES + nh] + dtb_ref[...])
    row = lax.broadcasted_iota(jnp.int32, (tm, tm), 0)
    col = lax.broadcasted_iota(jnp.int32, (tm, tm), 1)
    same = (row // CHUNK) == (col // CHUNK)
    tri_lo = jnp.where(same & (col <= row), 1.0, 0.0).astype(BF16)
    tri_up = jnp.where(same & (row <= col), 1.0, 0.0).astype(BF16)
    gc = sum(_dot(tri_lo, p) for p in _split3(g))
    bg_ref[0, :, 0:nh] = beta
    bg_ref[0, :, nh:2 * nh] = gc
    bat = _dot_nt(wbat_ref[...], hb)
    g_t = -jnp.exp(alogc_ref[...]) * _softplus(bat + dtbc_ref[...])
    gct = sum(_dot(p, tri_up) for p in _split3(g_t))
    for cc in range(nchunk):
        gt_ref[0, cc] = gct[:, cc * CHUNK:(cc + 1) * CHUNK]


def _gdn_in(x, mod, nw, w_main, w_ba, w_at, conv_w, a_log, dt_bias):
    B, S, D = x.shape
    tm = TM_PROJ
    nh = GDN_V_HEADS
    kern = functools.partial(_gdn_in_kernel, tm=tm)
    tok = lambda width: pl.BlockSpec((1, tm, width), lambda b, s: (b, s, 0))
    return pl.pallas_call(
        kern,
        grid=(B, S // tm),
        in_specs=[
            tok(D),
            pl.BlockSpec((1, 1, 3, D), lambda b, s: (b, 0, 0, 0)),
            _resident((1, D)),
            _resident(w_main.shape),
            _resident(w_ba.shape),
            _resident(w_at.shape),
            _resident(conv_w.shape),
            _resident((1, nh)), _resident((1, nh)), _resident((nh, 1)), _resident((nh, 1)),
        ],
        out_specs=[
            tok(GDN_QK_WIDTH), tok(GDN_QK_WIDTH),
            pl.BlockSpec((1, tm // CHUNK, GDN_QK_WIDTH, CHUNK), lambda b, s: (b, s, 0, 0)),
            tok(GDN_V_WIDTH), tok(GDN_V_WIDTH),
            tok(2 * nh),
            pl.BlockSpec((1, tm // CHUNK, nh, CHUNK), lambda b, s: (b, s, 0, 0)),
        ],
        out_shape=[
            jax.ShapeDtypeStruct((B, S, GDN_QK_WIDTH), BF16),
            jax.ShapeDtypeStruct((B, S, GDN_QK_WIDTH), BF16),
            jax.ShapeDtypeStruct((B, S // CHUNK, GDN_QK_WIDTH, CHUNK), BF16),
            jax.ShapeDtypeStruct((B, S, GDN_V_WIDTH), BF16),
            jax.ShapeDtypeStruct((B, S, GDN_V_WIDTH), BF16),
            jax.ShapeDtypeStruct((B, S, 2 * nh), F32),
            jax.ShapeDtypeStruct((B, S // CHUNK, nh, CHUNK), F32),
        ],
        scratch_shapes=[pltpu.VMEM((tm + SUBLANES, GDN_CONV_CH), F32)],
        compiler_params=_params("arbitrary", "arbitrary"),
        name="gdn_in",
    )(x, mod.reshape(B, 1, 3, D), nw.reshape(1, D), w_main, w_ba, w_at, conv_w,
      a_log.reshape(1, nh), dt_bias.reshape(1, nh), a_log.reshape(nh, 1), dt_bias.reshape(nh, 1))


def _delta_kernel(q_ref, k_ref, kt_ref, v_ref, z_ref, bg_ref, gt_ref, nw_ref, o_ref, s_ref, *, tc):
    s = pl.program_id(1)
    C = CHUNK
    dh = GDN_HEAD_DIM
    nqk = GDN_QK_HEADS
    nv = GDN_V_HEADS
    rep = nv // nqk
    heads = range(nv)

    @pl.when(s == 0)
    def _():
        s_ref[...] = jnp.zeros(s_ref.shape, F32)

    row = lax.broadcasted_iota(jnp.int32, (C, C), 0)
    col = lax.broadcasted_iota(jnp.int32, (C, C), 1)
    lower = row >= col
    strict = row > col
    eye = jnp.where(row == col, 1.0, 0.0).astype(F32)
    nw = nw_ref[...]

    def chunk(c, _):
        r0 = pl.multiple_of(c * C, C)
        rows = pl.ds(r0, C)
        bg = bg_ref[0, rows, :]
        gt = gt_ref[0, c]
        qb = [q_ref[0, rows, j * dh:(j + 1) * dh] for j in range(nqk)]
        kb = [k_ref[0, rows, j * dh:(j + 1) * dh] for j in range(nqk)]
        ktb = [kt_ref[0, c, j * dh:(j + 1) * dh, :] for j in range(nqk)]
        kk = [_dot(kb[j], ktb[j]) for j in range(nqk)]
        qk = [_dot(qb[j], ktb[j]) for j in range(nqk)]

        beta = [bg[:, h:h + 1] for h in heads]
        gc = [bg[:, nv + h:nv + h + 1] for h in heads]
        gr = [gt[h:h + 1, :] for h in heads]
        g_last = [gr[h][:, C - 1:C] for h in heads]
        decay = [jnp.exp(jnp.where(lower, gc[h] - gr[h], -jnp.inf)) for h in heads]
        egc = [jnp.exp(gc[h]) for h in heads]

        X = [-(jnp.where(strict, kk[h // rep] * decay[h], 0.0) * beta[h]) for h in heads]
        P = [eye + X[h] for h in heads]
        p = 2
        while p < C:
            Xb = [X[h].astype(BF16) for h in heads]
            X = [_dot(Xb[h], Xb[h]) for h in heads]
            P = [P[h] + _dot(P[h].astype(BF16), X[h].astype(BF16)) for h in heads]
            p *= 2

        rhs = []
        for h in heads:
            vf = v_ref[0, rows, h * dh:(h + 1) * dh].astype(F32)
            kf = kb[h // rep].astype(F32)
            rhs.append(jnp.concatenate([vf * beta[h], kf * (beta[h] * egc[h])], axis=1).astype(BF16))
        uw = [_dot(P[h].astype(BF16), rhs[h]) for h in heads]
        S = [s_ref[h] for h in heads]
        wq = [jnp.concatenate([uw[h][:, dh:], qb[h // rep].astype(F32) * egc[h]], axis=0).astype(BF16)
              for h in heads]
        r = [_dot(wq[h], S[h].astype(BF16)) for h in heads]
        vnb = [(uw[h][:, :dh] - r[h][:C]).astype(BF16) for h in heads]
        attn = [(qk[h // rep] * decay[h]).astype(BF16) for h in heads]
        o = [r[h][C:] + _dot(attn[h], vnb[h]) for h in heads]
        for h in heads:
            kdt = (ktb[h // rep].astype(F32) * jnp.exp(g_last[h] - gr[h])).astype(BF16)
            s_ref[h] = S[h] * jnp.exp(g_last[h]) + _dot(kdt, vnb[h])
        for h in heads:
            var = jnp.mean(o[h] * o[h], axis=-1, keepdims=True)
            zf = z_ref[0, rows, h * dh:(h + 1) * dh].astype(F32)
            o_ref[0, rows, h * dh:(h + 1) * dh] = (o[h] * lax.rsqrt(var + EPS) * nw * _silu(zf)).astype(BF16)
        return 0

    lax.fori_loop(0, tc // C, chunk, 0)


def _gdn_delta(q, k, kt, v, z, bg, gt, nw):
    B, S, _ = q.shape
    tc = TC_DELTA
    nv = GDN_V_HEADS
    dh = GDN_HEAD_DIM
    kern = functools.partial(_delta_kernel, tc=tc)
    tok = lambda width: pl.BlockSpec((1, tc, width), lambda b, s: (b, s, 0))
    return pl.pallas_call(
        kern,
        grid=(B, S // tc),
        in_specs=[
            tok(GDN_QK_WIDTH), tok(GDN_QK_WIDTH),
            pl.BlockSpec((1, tc // CHUNK, GDN_QK_WIDTH, CHUNK), lambda b, s: (b, s, 0, 0)),
            tok(GDN_V_WIDTH), tok(GDN_V_WIDTH),
            tok(2 * nv),
            pl.BlockSpec((1, tc // CHUNK, nv, CHUNK), lambda b, s: (b, s, 0, 0)),
            pl.BlockSpec((1, dh), lambda b, s: (0, 0)),
        ],
        out_specs=tok(GDN_V_WIDTH),
        out_shape=jax.ShapeDtypeStruct((B, S, GDN_V_WIDTH), BF16),
        scratch_shapes=[pltpu.VMEM((nv, dh, dh), F32)],
        compiler_params=_params("arbitrary", "arbitrary"),
        name="gdn_delta",
    )(q, k, kt, v, z, bg, gt, nw.reshape(1, dh))


def _fox_in_kernel(x_ref, mod_ref, nw_ref, wt_ref, wz_ref, wft_ref, fb_ref, qw_ref, kw_ref,
                   qt_ref, kt_ref, vt_ref, z_ref, cum_ref, carry_ref, *, tm):
    s = pl.program_id(1)
    W = FOX_WIDTH
    dh = FOX_HEAD_DIM

    @pl.when(s == 0)
    def _():
        carry_ref[...] = jnp.zeros(carry_ref.shape, F32)

    h = _modulated_norm(x_ref[0], mod_ref[0, 0], nw_ref[...])
    hb = h.astype(BF16)

    grp = FOX_GROUP
    for g in range(3 * W // grp):
        blk = _dot_nt(wt_ref[g * grp:(g + 1) * grp, :], hb)
        r0 = (g * grp) % W
        if g * grp < 2 * W:
            is_q = g * grp < W
            wcol = qw_ref[...] if is_q else kw_ref[...]
            dst = qt_ref if is_q else kt_ref
            for hh in range(grp // dh):
                seg = blk[hh * dh:(hh + 1) * dh]
                inv = lax.rsqrt(jnp.mean(seg * seg, axis=0, keepdims=True) + EPS)
                dst[0, r0 + hh * dh:r0 + (hh + 1) * dh, :] = (seg * inv * wcol).astype(BF16)
        else:
            vt_ref[0, r0:r0 + grp, :] = blk.astype(BF16)
    z_ref[0] = _dot(hb, wz_ref[...]).astype(BF16)

    xf = _dot_nt(wft_ref[...], hb) + fb_ref[...]
    log_f = -_softplus(-xf)
    row = lax.broadcasted_iota(jnp.int32, (tm, tm), 0)
    col = lax.broadcasted_iota(jnp.int32, (tm, tm), 1)
    tri_up = jnp.where(row <= col, 1.0, 0.0).astype(BF16)
    cum = sum(_dot(p, tri_up) for p in _split3(log_f)) + carry_ref[...]
    cum_ref[0] = cum
    carry_ref[...] = cum[:, tm - 1:tm]


def _fox_in(x, mod, nw, w_qkv_t, w_z, w_ft, f_bias, qw, kw):
    B, S, D = x.shape
    tm = TM_PROJ
    H = FOX_HEADS
    dh = FOX_HEAD_DIM
    W = FOX_WIDTH
    kern = functools.partial(_fox_in_kernel, tm=tm)
    feat_major = pl.BlockSpec((1, W, tm), lambda b, s: (b, 0, s))
    fm_shape = jax.ShapeDtypeStruct((B, W, S), BF16)
    return pl.pallas_call(
        kern,
        grid=(B, S // tm),
        in_specs=[
            pl.BlockSpec((1, tm, D), lambda b, s: (b, s, 0)),
            pl.BlockSpec((1, 1, 3, D), lambda b, s: (b, 0, 0, 0)),
            _resident((1, D)),
            _resident(w_qkv_t.shape),
            _resident(w_z.shape),
            _resident(w_ft.shape),
            _resident((H, 1)),
            _resident((dh, 1)),
            _resident((dh, 1)),
        ],
        out_specs=[
            feat_major, feat_major, feat_major,
            pl.BlockSpec((1, tm, W), lambda b, s: (b, s, 0)),
            pl.BlockSpec((1, H, tm), lambda b, s: (b, 0, s)),
        ],
        out_shape=[
            fm_shape, fm_shape, fm_shape,
            jax.ShapeDtypeStruct((B, S, W), BF16),
            jax.ShapeDtypeStruct((B, H, S), F32),
        ],
        scratch_shapes=[pltpu.VMEM((H, 1), F32)],
        compiler_params=_params("arbitrary", "arbitrary"),
        name="fox_in",
    )(x, mod.reshape(B, 1, 3, D), nw.reshape(1, D), w_qkv_t, w_z, w_ft, f_bias.reshape(H, 1),
      qw.reshape(dh, 1), kw.reshape(dh, 1))


def _fox_attn_kernel(qt_ref, kt_ref, vt_ref, z_ref, cum_ref, o_ref,
                     qa_ref, ka_ref, va_ref, r_ref, m_ref, acc_ref, st_ref, p_ref, *, tq, seq):
    hp = pl.program_id(1)
    dh = FOX_HEAD_DIM
    pair = LANES // dh
    nq = seq // tq
    nsplit = 3
    heads = range(pair)
    kidx = lax.broadcasted_iota(jnp.int32, (tq, tq), 0)
    qidx = lax.broadcasted_iota(jnp.int32, (tq, tq), 1)
    causal = kidx <= qidx

    aug_row = lax.broadcasted_iota(jnp.int32, (BF16_ROWS, seq), 0)
    pick = jnp.where(aug_row < nsplit, 1.0, 0.0).astype(BF16)
    one_row = jnp.where(aug_row < 1, 1.0, 0.0).astype(BF16)
    for e in heads:
        qa_ref[e, 0:dh, :] = qt_ref[0, e * dh:(e + 1) * dh, :]
        qa_ref[e, dh:dh + BF16_ROWS, :] = pick
        qa_ref[e, dh + BF16_ROWS:, :] = jnp.zeros((LANES - dh - BF16_ROWS, seq), BF16)
        va_ref[e, 0:dh, :] = vt_ref[0, e * dh:(e + 1) * dh, :]
        va_ref[e, dh:, :] = one_row
        crow = cum_ref[0, pl.ds(pair * hp + e, 1), :]
        firsts = [jnp.broadcast_to(crow[:, j * tq:j * tq + 1], (1, tq)) for j in range(nq)]
        for j in range(nq):
            r_ref[e, j] = firsts[j] * (-LOG2E)
        rel = (crow - jnp.concatenate(firsts, axis=1)) * (-LOG2E)
        parts = [p.astype(F32) for p in _split3(rel)]
        btile = jnp.concatenate(parts + [jnp.zeros((SUBLANES - nsplit, seq), F32)], axis=0)
        for blk in range(seq // KT_BLOCK):
            sl = slice(blk * KT_BLOCK, (blk + 1) * KT_BLOCK)
            top = jnp.concatenate([kt_ref[0, e * dh:(e + 1) * dh, sl].astype(F32), btile[:, sl],
                                   jnp.zeros((LANES - dh - SUBLANES, KT_BLOCK), F32)], axis=0)
            ka_ref[e, sl, :] = top.T.astype(BF16)

    m_ref[...] = jnp.full(m_ref.shape, -jnp.inf, F32)
    acc_ref[...] = jnp.zeros(acc_ref.shape, F32)
    p_ref[...] = jnp.zeros(p_ref.shape, BF16)

    def scores(e, qi, kj):
        return _dot(ka_ref[e, pl.ds(pl.multiple_of(kj * tq, tq), tq), :],
                    qa_ref[e, :, pl.ds(pl.multiple_of(qi * tq, tq), tq)])

    def weighted_values(e, kj, p):
        return _dot(va_ref[e, :, pl.ds(pl.multiple_of(kj * tq, tq), tq)], p)

    def softmax_update(e, qi, kj, st):
        off = r_ref[e, kj]
        m_rel = m_ref[e, qi] - off
        m_new = jnp.maximum(m_rel, jnp.max(st, axis=0, keepdims=True))
        m_ref[e, qi] = m_new + off
        return jnp.exp2(m_rel - m_new), jnp.exp2(st - m_new).astype(BF16)

    def mask_diagonal(st, masked):
        return jnp.where(causal, st, -jnp.inf) if masked else st

    def simple_step(pr, masked):
        qi, kj = pr
        for e in heads:
            alpha, p = softmax_update(e, qi, kj, mask_diagonal(scores(e, qi, kj), masked))
            acc_ref[e, qi] = alpha * acc_ref[e, qi] + weighted_values(e, kj, p)

    def sweep(first, advance, npairs, masked):
        unroll = ATTN_UNROLL
        n_main = (npairs // unroll) * unroll
        cur = first
        if n_main:
            second = advance(*first)
            for e in heads:
                st_ref[e, 0] = scores(e, *first)
                st_ref[e, 1] = scores(e, *second)

            def body(_, carry):
                prev, cur, nxt, a_prev = carry
                for i in range(unroll):
                    nxt2 = advance(*nxt)
                    new_a = []
                    for e in heads:
                        st = mask_diagonal(st_ref[e, i % ST_RING], masked)
                        pv_prev = weighted_values(e, prev[1], p_ref[e, (i - 1) % P_RING])
                        st_ref[e, (i + 2) % ST_RING] = scores(e, *nxt2)
                        alpha, p = softmax_update(e, cur[0], cur[1], st)
                        p_ref[e, i % P_RING] = p
                        acc_ref[e, prev[0]] = a_prev[e] * acc_ref[e, prev[0]] + pv_prev
                        new_a.append(alpha)
                    prev, cur, nxt, a_prev = cur, nxt, nxt2, tuple(new_a)
                return prev, cur, nxt, a_prev

            ones = tuple(jnp.ones((1, tq), F32) for _ in heads)
            spare = (jnp.int32(nq), jnp.int32(0))
            prev, cur, _, a_prev = lax.fori_loop(0, n_main // unroll, body, (spare, first, second, ones))
            for e in heads:
                pv_prev = weighted_values(e, prev[1], p_ref[e, (unroll - 1) % P_RING])
                acc_ref[e, prev[0]] = a_prev[e] * acc_ref[e, prev[0]] + pv_prev

        def tail(_, pr):
            simple_step(pr, masked)
            return advance(*pr)

        lax.fori_loop(0, npairs - n_main, tail, cur)

    def next_off_diagonal(qi, kj):
        k2 = kj + 1
        wrap = k2 >= qi
        return jnp.where(wrap, jnp.minimum(qi + 1, nq - 1), qi), jnp.where(wrap, 0, k2)

    def next_diagonal(qi, kj):
        nxt = jnp.minimum(qi + 1, nq - 1)
        return nxt, nxt

    zero = jnp.int32(0)
    sweep((jnp.int32(min(1, nq - 1)), zero), next_off_diagonal, nq * (nq - 1) // 2, False)
    sweep((zero, zero), next_diagonal, nq, True)

    def finish(qi, _):
        accs = [acc_ref[e, qi] for e in heads]
        o = jnp.concatenate([a[0:dh] * (1.0 / a[dh:dh + 1]) for a in accs], axis=0).T
        rows = pl.ds(pl.multiple_of(qi * tq, tq), tq)
        zf = z_ref[0, rows, :].astype(F32)
        o_ref[0, rows, :] = (o * _silu(zf)).astype(BF16)
        return 0

    lax.fori_loop(0, nq, finish, 0)


def _fox_attn(qt, kt, vt, z, cum):
    B, W, S = qt.shape
    H = FOX_HEADS
    dh = FOX_HEAD_DIM
    pair = LANES // dh
    tq = TQ_ATTN
    nq = S // tq
    kern = functools.partial(_fox_attn_kernel, tq=tq, seq=S)
    fm = pl.BlockSpec((1, LANES, S), lambda b, h: (b, h, 0))
    return pl.pallas_call(
        kern,
        grid=(B, H // pair),
        in_specs=[
            fm, fm, fm,
            pl.BlockSpec((1, S, LANES), lambda b, h: (b, 0, h)),
            pl.BlockSpec((1, H, S), lambda b, h: (b, 0, 0)),
        ],
        out_specs=pl.BlockSpec((1, S, LANES), lambda b, h: (b, 0, h)),
        out_shape=jax.ShapeDtypeStruct((B, S, W), BF16),
        scratch_shapes=[
            pltpu.VMEM((pair, LANES, S), BF16),
            pltpu.VMEM((pair, S, LANES), BF16),
            pltpu.VMEM((pair, dh + BF16_ROWS, S), BF16),
            pltpu.VMEM((pair, nq, 1, tq), F32),
            pltpu.VMEM((pair, nq, 1, tq), F32),
            pltpu.VMEM((pair, nq + 1, dh + BF16_ROWS, tq), F32),
            pltpu.VMEM((pair, ST_RING, tq, tq), F32),
            pltpu.VMEM((pair, P_RING, tq, tq), BF16),
        ],
        compiler_params=_params("arbitrary", "arbitrary"),
        name="fox_attn",
    )(qt, kt, vt, z, cum)


def _out_kernel(o_ref, x_ref, mod_ref, w_ref, fnw_ref, y_ref, *, final):
    y = _dot(o_ref[0], w_ref[...])
    xn = x_ref[0] + mod_ref[0, 0][2:3, :] * y
    if final:
        ms = jnp.mean(xn * xn, axis=-1, keepdims=True)
        xn = xn * lax.rsqrt(ms + EPS) * fnw_ref[...]
    y_ref[0] = xn


def _out_proj(o, x, mod, w_out, fnw, final):
    B, S, D = x.shape
    width = o.shape[-1]
    tm = TM_PROJ
    kern = functools.partial(_out_kernel, final=final)
    return pl.pallas_call(
        kern,
        grid=(B, S // tm),
        in_specs=[
            pl.BlockSpec((1, tm, width), lambda b, s: (b, s, 0)),
            pl.BlockSpec((1, tm, D), lambda b, s: (b, s, 0)),
            pl.BlockSpec((1, 1, 3, D), lambda b, s: (b, 0, 0, 0)),
            _resident(w_out.shape),
            _resident((1, D)),
        ],
        out_specs=pl.BlockSpec((1, tm, D), lambda b, s: (b, s, 0)),
        out_shape=jax.ShapeDtypeStruct((B, S, D), F32),
        compiler_params=_params("arbitrary", "arbitrary"),
        name="out_proj",
    )(o, x, mod.reshape(B, 1, 3, D), w_out, fnw.reshape(1, D))


def kernel(x, c, norm_w, ada_w, ada_b, a_w_in, a_conv_w, a_A_log, a_dt_bias, a_norm_w, a_w_out,
           b_w_in, b_f_bias, b_qn_w, b_kn_w, b_w_out, final_norm_w):
    B, S, D = x.shape
    assert D == D_MODEL and S % max(TM_PROJ, TC_DELTA, TQ_ATTN) == 0
    mods = _ada_mod(c, ada_w, ada_b)
    nh = GDN_V_HEADS
    for i in range(DEPTH):
        j = i // 2
        final = i == DEPTH - 1
        if i % 2 == 0:
            w_in = a_w_in[j]
            w_main = w_in[:, :GDN_CONV_CH + GDN_V_WIDTH].astype(BF16)
            w_b = w_in[:, GDN_CONV_CH + GDN_V_WIDTH:GDN_CONV_CH + GDN_V_WIDTH + nh]
            w_a = w_in[:, GDN_CONV_CH + GDN_V_WIDTH + nh:]
            pad = jnp.zeros((D, LANES - nh), F32)
            w_ba = jnp.concatenate([w_b, pad, w_a, pad], axis=1).astype(BF16)
            w_at = w_a.T.astype(BF16)
            q, k, kt, v, z, bg, gt = _gdn_in(x, mods[i], norm_w[i], w_main, w_ba, w_at, a_conv_w[j],
                                              a_A_log[j], a_dt_bias[j])
            o = _gdn_delta(q, k, kt, v, z, bg, gt, a_norm_w[j])
            w_out = a_w_out[j].astype(BF16)
        else:
            w_in = b_w_in[j]
            w_qkv_t = w_in[:, :3 * FOX_WIDTH].T.astype(BF16)
            w_z = w_in[:, 3 * FOX_WIDTH:4 * FOX_WIDTH].astype(BF16)
            w_ft = w_in[:, 4 * FOX_WIDTH:].T.astype(BF16)
            qw = b_qn_w[j] * (FOX_HEAD_DIM ** -0.5 * LOG2E)
            qt, kt, vt, z, cum = _fox_in(x, mods[i], norm_w[i], w_qkv_t, w_z, w_ft, b_f_bias[j], qw, b_kn_w[j])
            o = _fox_attn(qt, kt, vt, z, cum)
            w_out = b_w_out[j].astype(BF16)
        x = _out_proj(o, x, mods[i], w_out, final_norm_w, final)
    return x
```

```python
import functools

import jax
import jax.numpy as jnp
from jax import lax
from jax.experimental import pallas as pl
from jax.experimental.pallas import tpu as pltpu

F32 = jnp.float32
BF16 = jnp.bfloat16

D_MODEL = 1024
DEPTH = 4
EPS = 1e-6
CHUNK = 64

GDN_QK_HEADS = 8
GDN_V_HEADS = 16
GDN_HEAD_DIM = 128
GDN_QK_WIDTH = GDN_QK_HEADS * GDN_HEAD_DIM
GDN_V_WIDTH = GDN_V_HEADS * GDN_HEAD_DIM
GDN_CONV_CH = 2 * GDN_QK_WIDTH + GDN_V_WIDTH
CONV_WIDTH = 4

FOX_HEADS = 16
FOX_HEAD_DIM = 64
FOX_WIDTH = FOX_HEADS * FOX_HEAD_DIM

LANES = 128
SUBLANES = 8
BF16_ROWS = 16
LOG2E = 1.4426950408889634
VMEM_LIMIT = 48 * 1024 * 1024

TM_PROJ = 512
TM_GDN_IN = 256
TC_DELTA = 256
DELTA_CHUNKS_PER_TRIP = 2
TQ_ATTN = 256
CONV_GROUP = 512
FOX_GROUP = 512
KT_BLOCK = 512
ATTN_UNROLL = 8
ST_RING = 4
P_RING = 2


def _sigmoid(x):
    return 1.0 / (1.0 + jnp.exp(-x))


def _silu(x):
    return x * _sigmoid(x)


def _softplus(x):
    return jnp.maximum(x, 0.0) + jnp.log(1.0 + jnp.exp(-jnp.abs(x)))


def _split3(a):
    hi = a.astype(BF16)
    r = a - hi.astype(F32)
    mid = r.astype(BF16)
    lo = (r - mid.astype(F32)).astype(BF16)
    return hi, mid, lo


def _dot(a, b):
    return jnp.dot(a, b, preferred_element_type=F32)


def _dot_nt(a, b):
    return lax.dot_general(a, b, (((1,), (1,)), ((), ())), preferred_element_type=F32)


def _dot_tn(a, b):
    return lax.dot_general(a, b, (((0,), (0,)), ((), ())), preferred_element_type=F32)


def _modulated_norm(x, mod, nw):
    ms = jnp.mean(x * x, axis=-1, keepdims=True)
    y = x * lax.rsqrt(ms + EPS) * nw
    return y * (1.0 + mod[1:2, :]) + mod[0:1, :]


def _params(*sem):
    return pltpu.CompilerParams(dimension_semantics=sem, vmem_limit_bytes=VMEM_LIMIT)


def _resident(shape):
    nd = len(shape)
    return pl.BlockSpec(shape, lambda *_: (0,) * nd, pipeline_mode=pl.Buffered(1))


def _ada_kernel(c_ref, w_ref, b_ref, o_ref):
    cond = _silu(c_ref[...])
    o_ref[0] = _dot(cond, w_ref[0]) + b_ref[0]


def _ada_mod(c, ada_w, ada_b):
    B, D = c.shape
    depth = ada_w.shape[0]
    out = pl.pallas_call(
        _ada_kernel,
        grid=(depth,),
        in_specs=[
            pl.BlockSpec((B, D), lambda i: (0, 0)),
            pl.BlockSpec((1, D, 3 * D), lambda i: (i, 0, 0)),
            pl.BlockSpec((1, 1, 3 * D), lambda i: (i, 0, 0)),
        ],
        out_specs=pl.BlockSpec((1, B, 3 * D), lambda i: (i, 0, 0)),
        out_shape=jax.ShapeDtypeStruct((depth, B, 3 * D), F32),
        compiler_params=_params("arbitrary"),
        name="ada_mod",
    )(c, ada_w, ada_b.reshape(depth, 1, 3 * D))
    return out.reshape(depth, B, 3, D)


def _gdn_in_kernel(x_ref, mod_ref, nw_ref, w_ref, wba_ref, wbat_ref, cw_ref,
                   alog_ref, dtb_ref, alogc_ref, dtbc_ref,
                   q_ref, k_ref, kt_ref, v_ref, z_ref, bg_ref, gt_ref, cbuf_ref, *, tm):
    s = pl.program_id(1)
    tail = SUBLANES
    nchunk = tm // CHUNK

    @pl.when(s == 0)
    def _():
        cbuf_ref[0:tail, :] = jnp.zeros((tail, GDN_CONV_CH), F32)

    h = _modulated_norm(x_ref[0], mod_ref[0, 0], nw_ref[...])
    hb = h.astype(BF16)

    qscale = GDN_HEAD_DIM ** -0.5
    for c in range(GDN_CONV_CH // CONV_GROUP):
        lo = c * CONV_GROUP
        pre = _dot(hb, w_ref[:, lo:lo + CONV_GROUP])
        cbuf_ref[tail:tail + tm, lo:lo + CONV_GROUP] = pre
        acc = cw_ref[CONV_WIDTH - 1:CONV_WIDTH, lo:lo + CONV_GROUP] * pre
        for j in range(CONV_WIDTH - 1):
            off = tail - (CONV_WIDTH - 1) + j
            acc = acc + cw_ref[j:j + 1, lo:lo + CONV_GROUP] * cbuf_ref[off:off + tm, lo:lo + CONV_GROUP]
        y = _silu(acc)
        if lo < 2 * GDN_QK_WIDTH:
            for hh in range(CONV_GROUP // GDN_HEAD_DIM):
                seg = y[:, hh * GDN_HEAD_DIM:(hh + 1) * GDN_HEAD_DIM]
                inv = lax.rsqrt(jnp.sum(seg * seg, axis=-1, keepdims=True) + EPS)
                col = lo + hh * GDN_HEAD_DIM
                if col < GDN_QK_WIDTH:
                    q_ref[0, :, col:col + GDN_HEAD_DIM] = (seg * inv * qscale).astype(BF16)
                else:
                    col -= GDN_QK_WIDTH
                    kn = seg * inv
                    k_ref[0, :, col:col + GDN_HEAD_DIM] = kn.astype(BF16)
                    knt = kn.T.astype(BF16)
                    for cc in range(nchunk):
                        kt_ref[0, cc, col:col + GDN_HEAD_DIM, :] = knt[:, cc * CHUNK:(cc + 1) * CHUNK]
        else:
            col = lo - 2 * GDN_QK_WIDTH
            v_ref[0, :, col:col + CONV_GROUP] = y.astype(BF16)
    cbuf_ref[0:tail, :] = cbuf_ref[tm:tm + tail, :]

    for c in range(GDN_V_WIDTH // CONV_GROUP):
        lo = c * CONV_GROUP
        z = _dot(hb, w_ref[:, GDN_CONV_CH + lo:GDN_CONV_CH + lo + CONV_GROUP])
        z_ref[0, :, lo:lo + CONV_GROUP] = z.astype(BF16)

    nh = GDN_V_HEADS
    ba = _dot(hb, wba_ref[...])
    beta = _sigmoid(ba[:, 0:nh])
    g = -jnp.exp(alog_ref[...]) * _softplus(ba[:, LANES:LANES + nh] + dtb_ref[...])
    row = lax.broadcasted_iota(jnp.int32, (tm, tm), 0)
    col = lax.broadcasted_iota(jnp.int32, (tm, tm), 1)
    same = (row // CHUNK) == (col // CHUNK)
    tri_lo = jnp.where(same & (col <= row), 1.0, 0.0).astype(BF16)
    tri_up = jnp.where(same & (row <= col), 1.0, 0.0).astype(BF16)
    gc = sum(_dot(tri_lo, p) for p in _split3(g))
    bg_ref[0, :, 0:nh] = beta
    bg_ref[0, :, nh:2 * nh] = gc
    bat = _dot_nt(wbat_ref[...], hb)
    g_t = -jnp.exp(alogc_ref[...]) * _softplus(bat + dtbc_ref[...])
    gct = sum(_dot(p, tri_up) for p in _split3(g_t))
    for cc in range(nchunk):
        gt_ref[0, cc] = gct[:, cc * CHUNK:(cc + 1) * CHUNK]


def _gdn_in(x, mod, nw, w_main, w_ba, w_at, conv_w, a_log, dt_bias):
    B, S, D = x.shape
    tm = TM_GDN_IN
    nh = GDN_V_HEADS
    kern = functools.partial(_gdn_in_kernel, tm=tm)
    tok = lambda width: pl.BlockSpec((1, tm, width), lambda b, s: (b, s, 0))
    return pl.pallas_call(
        kern,
        grid=(B, S // tm),
        in_specs=[
            tok(D),
            pl.BlockSpec((1, 1, 3, D), lambda b, s: (b, 0, 0, 0)),
            _resident((1, D)),
            _resident(w_main.shape),
            _resident(w_ba.shape),
            _resident(w_at.shape),
            _resident(conv_w.shape),
            _resident((1, nh)), _resident((1, nh)), _resident((nh, 1)), _resident((nh, 1)),
        ],
        out_specs=[
            tok(GDN_QK_WIDTH), tok(GDN_QK_WIDTH),
            pl.BlockSpec((1, tm // CHUNK, GDN_QK_WIDTH, CHUNK), lambda b, s: (b, s, 0, 0)),
            tok(GDN_V_WIDTH), tok(GDN_V_WIDTH),
            tok(2 * nh),
            pl.BlockSpec((1, tm // CHUNK, nh, CHUNK), lambda b, s: (b, s, 0, 0)),
        ],
        out_shape=[
            jax.ShapeDtypeStruct((B, S, GDN_QK_WIDTH), BF16),
            jax.ShapeDtypeStruct((B, S, GDN_QK_WIDTH), BF16),
            jax.ShapeDtypeStruct((B, S // CHUNK, GDN_QK_WIDTH, CHUNK), BF16),
            jax.ShapeDtypeStruct((B, S, GDN_V_WIDTH), BF16),
            jax.ShapeDtypeStruct((B, S, GDN_V_WIDTH), BF16),
            jax.ShapeDtypeStruct((B, S, 2 * nh), F32),
            jax.ShapeDtypeStruct((B, S // CHUNK, nh, CHUNK), F32),
        ],
        scratch_shapes=[pltpu.VMEM((tm + SUBLANES, GDN_CONV_CH), F32)],
        compiler_params=_params("arbitrary", "arbitrary"),
        name="gdn_in",
    )(x, mod.reshape(B, 1, 3, D), nw.reshape(1, D), w_main, w_ba, w_at, conv_w,
      a_log.reshape(1, nh), dt_bias.reshape(1, nh), a_log.reshape(nh, 1), dt_bias.reshape(nh, 1))


def _delta_kernel(q_ref, k_ref, kt_ref, v_ref, z_ref, bg_ref, gt_ref, nw_ref, o_ref, s_ref, *, tc):
    s = pl.program_id(1)
    C = CHUNK
    dh = GDN_HEAD_DIM
    nqk = GDN_QK_HEADS
    nv = GDN_V_HEADS
    rep = nv // nqk
    heads = range(nv)

    @pl.when(s == 0)
    def _():
        s_ref[...] = jnp.zeros(s_ref.shape, F32)

    row = lax.broadcasted_iota(jnp.int32, (C, C), 0)
    col = lax.broadcasted_iota(jnp.int32, (C, C), 1)
    lower = row >= col
    strict = row > col
    eye = jnp.where(row == col, 1.0, 0.0).astype(F32)
    nw = nw_ref[...]

    nper = DELTA_CHUNKS_PER_TRIP
    units = [(t, h) for t in range(nper) for h in heads]

    def trip(i, _):
        cs = [i * nper + t for t in range(nper)]
        rows = [pl.ds(pl.multiple_of(c * C, C), C) for c in cs]
        bg = [bg_ref[0, rows[t], :] for t in range(nper)]
        gt = [gt_ref[0, cs[t]] for t in range(nper)]
        groups = [(t, j) for t in range(nper) for j in range(nqk)]
        qb = {(t, j): q_ref[0, rows[t], j * dh:(j + 1) * dh] for t, j in groups}
        kb = {(t, j): k_ref[0, rows[t], j * dh:(j + 1) * dh] for t, j in groups}
        ktb = {(t, j): kt_ref[0, cs[t], j * dh:(j + 1) * dh, :] for t, j in groups}
        kq = {g: _dot(jnp.concatenate([kb[g], qb[g]], axis=0), ktb[g]) for g in groups}
        grp = {(t, h): (t, h // rep) for t, h in units}

        beta = {(t, h): bg[t][:, h:h + 1] for t, h in units}
        gc = {(t, h): bg[t][:, nv + h:nv + h + 1] for t, h in units}
        gr = {(t, h): gt[t][h:h + 1, :] for t, h in units}
        g_last = {u: gr[u][:, C - 1:C] for u in units}
        decay = {u: jnp.exp(jnp.where(lower, gc[u] - gr[u], -jnp.inf)) for u in units}
        egc = {u: jnp.exp(gc[u]) for u in units}

        X = {u: -(jnp.where(strict, kq[grp[u]][:C] * decay[u], 0.0) * beta[u]) for u in units}
        P = {u: eye + X[u] for u in units}
        Xb = {u: X[u].astype(BF16) for u in units}
        X = {u: _dot(Xb[u], Xb[u]) for u in units}
        p = 4
        while p < C:
            Xb = {u: X[u].astype(BF16) for u in units}
            PX = {u: _dot(jnp.concatenate([P[u], X[u]], axis=0).astype(BF16), Xb[u]) for u in units}
            P = {u: P[u] + PX[u][:C] for u in units}
            X = {u: PX[u][C:] for u in units}
            p *= 2
        P = {u: P[u] + _dot(P[u].astype(BF16), X[u].astype(BF16)) for u in units}

        uw = {}
        for t, h in units:
            u = (t, h)
            vf = v_ref[0, rows[t], h * dh:(h + 1) * dh].astype(F32)
            kf = kb[grp[u]].astype(F32)
            rhs = jnp.concatenate([vf * beta[u], kf * (beta[u] * egc[u])], axis=1).astype(BF16)
            uw[u] = _dot(P[u].astype(BF16), rhs)
        wq = {u: jnp.concatenate([uw[u][:, dh:], qb[grp[u]].astype(F32) * egc[u]], axis=0).astype(BF16)
              for u in units}
        ak_lhs = {}
        for u in units:
            attn = (kq[grp[u]][C:] * decay[u]).astype(BF16)
            kdt = (ktb[grp[u]].astype(F32) * jnp.exp(g_last[u] - gr[u])).astype(BF16)
            ak_lhs[u] = jnp.concatenate([attn, kdt], axis=0)
        for t in range(nper):
            S = [s_ref[h] for h in heads]
            r = [_dot(wq[(t, h)], S[h].astype(BF16)) for h in heads]
            vnb = [(uw[(t, h)][:, :dh] - r[h][:C]).astype(BF16) for h in heads]
            ak = [_dot(ak_lhs[(t, h)], vnb[h]) for h in heads]
            for h in heads:
                s_ref[h] = S[h] * jnp.exp(g_last[(t, h)]) + ak[h][C:]
            for h in heads:
                o = r[h][C:] + ak[h][:C]
                var = jnp.mean(o * o, axis=-1, keepdims=True)
                zf = z_ref[0, rows[t], h * dh:(h + 1) * dh].astype(F32)
                o_ref[0, rows[t], h * dh:(h + 1) * dh] = (o * lax.rsqrt(var + EPS) * nw * _silu(zf)).astype(BF16)
        return 0

    lax.fori_loop(0, tc // (C * nper), trip, 0)


def _gdn_delta(q, k, kt, v, z, bg, gt, nw):
    B, S, _ = q.shape
    tc = TC_DELTA
    nv = GDN_V_HEADS
    dh = GDN_HEAD_DIM
    kern = functools.partial(_delta_kernel, tc=tc)
    tok = lambda width: pl.BlockSpec((1, tc, width), lambda b, s: (b, s, 0))
    return pl.pallas_call(
        kern,
        grid=(B, S // tc),
        in_specs=[
            tok(GDN_QK_WIDTH), tok(GDN_QK_WIDTH),
            pl.BlockSpec((1, tc // CHUNK, GDN_QK_WIDTH, CHUNK), lambda b, s: (b, s, 0, 0)),
            tok(GDN_V_WIDTH), tok(GDN_V_WIDTH),
            tok(2 * nv),
            pl.BlockSpec((1, tc // CHUNK, nv, CHUNK), lambda b, s: (b, s, 0, 0)),
            pl.BlockSpec((1, dh), lambda b, s: (0, 0)),
        ],
        out_specs=tok(GDN_V_WIDTH),
        out_shape=jax.ShapeDtypeStruct((B, S, GDN_V_WIDTH), BF16),
        scratch_shapes=[pltpu.VMEM((nv, dh, dh), F32)],
        compiler_params=_params("arbitrary", "arbitrary"),
        name="gdn_delta",
    )(q, k, kt, v, z, bg, gt, nw.reshape(1, dh))


def _fox_in_kernel(x_ref, mod_ref, nw_ref, wt_ref, wz_ref, wft_ref, fb_ref, qw_ref, kw_ref,
                   qt_ref, kt_ref, vt_ref, z_ref, cum_ref, carry_ref, *, tm):
    s = pl.program_id(1)
    W = FOX_WIDTH
    dh = FOX_HEAD_DIM

    @pl.when(s == 0)
    def _():
        carry_ref[...] = jnp.zeros(carry_ref.shape, F32)

    h = _modulated_norm(x_ref[0], mod_ref[0, 0], nw_ref[...])
    hb = h.astype(BF16)

    grp = FOX_GROUP
    for g in range(3 * W // grp):
        blk = _dot_nt(wt_ref[g * grp:(g + 1) * grp, :], hb)
        r0 = (g * grp) % W
        if g * grp < 2 * W:
            is_q = g * grp < W
            wcol = qw_ref[...] if is_q else kw_ref[...]
            dst = qt_ref if is_q else kt_ref
            for hh in range(grp // dh):
                seg = blk[hh * dh:(hh + 1) * dh]
                inv = lax.rsqrt(jnp.mean(seg * seg, axis=0, keepdims=True) + EPS)
                dst[0, r0 + hh * dh:r0 + (hh + 1) * dh, :] = (seg * inv * wcol).astype(BF16)
        else:
            vt_ref[0, r0:r0 + grp, :] = blk.astype(BF16)
    z_ref[0] = _dot(hb, wz_ref[...]).astype(BF16)

    xf = _dot_nt(wft_ref[...], hb) + fb_ref[...]
    log_f = -_softplus(-xf)
    row = lax.broadcasted_iota(jnp.int32, (tm, tm), 0)
    col = lax.broadcasted_iota(jnp.int32, (tm, tm), 1)
    tri_up = jnp.where(row <= col, 1.0, 0.0).astype(BF16)
    cum = sum(_dot(p, tri_up) for p in _split3(log_f)) + carry_ref[...]
    cum_ref[0] = cum
    carry_ref[...] = cum[:, tm - 1:tm]


def _fox_in(x, mod, nw, w_qkv_t, w_z, w_ft, f_bias, qw, kw):
    B, S, D = x.shape
    tm = TM_PROJ
    H = FOX_HEADS
    dh = FOX_HEAD_DIM
    W = FOX_WIDTH
    kern = functools.partial(_fox_in_kernel, tm=tm)
    feat_major = pl.BlockSpec((1, W, tm), lambda b, s: (b, 0, s))
    fm_shape = jax.ShapeDtypeStruct((B, W, S), BF16)
    return pl.pallas_call(
        kern,
        grid=(B, S // tm),
        in_specs=[
            pl.BlockSpec((1, tm, D), lambda b, s: (b, s, 0)),
            pl.BlockSpec((1, 1, 3, D), lambda b, s: (b, 0, 0, 0)),
            _resident((1, D)),
            _resident(w_qkv_t.shape),
            _resident(w_z.shape),
            _resident(w_ft.shape),
            _resident((H, 1)),
            _resident((dh, 1)),
            _resident((dh, 1)),
        ],
        out_specs=[
            feat_major, feat_major, feat_major,
            pl.BlockSpec((1, tm, W), lambda b, s: (b, s, 0)),
            pl.BlockSpec((1, H, tm), lambda b, s: (b, 0, s)),
        ],
        out_shape=[
            fm_shape, fm_shape, fm_shape,
            jax.ShapeDtypeStruct((B, S, W), BF16),
            jax.ShapeDtypeStruct((B, H, S), F32),
        ],
        scratch_shapes=[pltpu.VMEM((H, 1), F32)],
        compiler_params=_params("arbitrary", "arbitrary"),
        name="fox_in",
    )(x, mod.reshape(B, 1, 3, D), nw.reshape(1, D), w_qkv_t, w_z, w_ft, f_bias.reshape(H, 1),
      qw.reshape(dh, 1), kw.reshape(dh, 1))


def _fox_attn_kernel(qt_ref, kt_ref, vt_ref, z_ref, cum_ref, o_ref,
                     qa_ref, ka_ref, va_ref, r_ref, m_ref, acc_ref, st_ref, p_ref, *, tq, seq):
    hp = pl.program_id(1)
    dh = FOX_HEAD_DIM
    pair = LANES // dh
    nq = seq // tq
    nsplit = 3
    heads = range(pair)
    kidx = lax.broadcasted_iota(jnp.int32, (tq, tq), 0)
    qidx = lax.broadcasted_iota(jnp.int32, (tq, tq), 1)
    causal = kidx <= qidx

    aug_row = lax.broadcasted_iota(jnp.int32, (BF16_ROWS, seq), 0)
    pick = jnp.where(aug_row < nsplit, 1.0, 0.0).astype(BF16)
    one_row = jnp.where(aug_row < 1, 1.0, 0.0).astype(BF16)
    for e in heads:
        qa_ref[e, 0:dh, :] = qt_ref[0, e * dh:(e + 1) * dh, :]
        qa_ref[e, dh:dh + BF16_ROWS, :] = pick
        qa_ref[e, dh + BF16_ROWS:, :] = jnp.zeros((LANES - dh - BF16_ROWS, seq), BF16)
        va_ref[e, 0:dh, :] = vt_ref[0, e * dh:(e + 1) * dh, :]
        va_ref[e, dh:, :] = one_row
        crow = cum_ref[0, pl.ds(pair * hp + e, 1), :]
        firsts = [jnp.broadcast_to(crow[:, j * tq:j * tq + 1], (1, tq)) for j in range(nq)]
        for j in range(nq):
            r_ref[e, j] = firsts[j] * (-LOG2E)
        rel = (crow - jnp.concatenate(firsts, axis=1)) * (-LOG2E)
        parts = [p.astype(F32) for p in _split3(rel)]
        btile = jnp.concatenate(parts + [jnp.zeros((SUBLANES - nsplit, seq), F32)], axis=0)
        for blk in range(seq // KT_BLOCK):
            sl = slice(blk * KT_BLOCK, (blk + 1) * KT_BLOCK)
            top = jnp.concatenate([kt_ref[0, e * dh:(e + 1) * dh, sl].astype(F32), btile[:, sl],
                                   jnp.zeros((LANES - dh - SUBLANES, KT_BLOCK), F32)], axis=0)
            ka_ref[e, sl, :] = top.T.astype(BF16)

    m_ref[...] = jnp.full(m_ref.shape, -jnp.inf, F32)
    acc_ref[...] = jnp.zeros(acc_ref.shape, F32)
    p_ref[...] = jnp.zeros(p_ref.shape, BF16)

    def scores(e, qi, kj):
        return _dot(ka_ref[e, pl.ds(pl.multiple_of(kj * tq, tq), tq), :],
                    qa_ref[e, :, pl.ds(pl.multiple_of(qi * tq, tq), tq)])

    def weighted_values(e, kj, p):
        return _dot(va_ref[e, :, pl.ds(pl.multiple_of(kj * tq, tq), tq)], p)

    def softmax_update(e, qi, kj, read_scores):
        off = r_ref[e, kj]
        m_rel = m_ref[e, qi] - off
        m_new = jnp.maximum(m_rel, jnp.max(read_scores(), axis=0, keepdims=True))
        m_ref[e, qi] = m_new + off
        return jnp.exp2(m_rel - m_new), jnp.exp2(read_scores() - m_new).astype(BF16)

    def mask_diagonal(st, masked):
        return jnp.where(causal, st, -jnp.inf) if masked else st

    def simple_step(pr, masked):
        qi, kj = pr
        for e in heads:
            st = mask_diagonal(scores(e, qi, kj), masked)
            alpha, p = softmax_update(e, qi, kj, lambda st=st: st)
            acc_ref[e, qi] = alpha * acc_ref[e, qi] + weighted_values(e, kj, p)

    def sweep(first, advance, npairs, masked):
        unroll = ATTN_UNROLL
        n_main = (npairs // unroll) * unroll
        cur = first
        if n_main:
            second = advance(*first)
            for e in heads:
                st_ref[e, 0] = scores(e, *first)
                st_ref[e, 1] = scores(e, *second)

            def body(_, carry):
                prev, cur, nxt, a_prev = carry
                for i in range(unroll):
                    nxt2 = advance(*nxt)
                    new_a = []
                    for e in heads:
                        pv_prev = weighted_values(e, prev[1], p_ref[e, (i - 1) % P_RING])
                        st_ref[e, (i + 2) % ST_RING] = scores(e, *nxt2)
                        alpha, p = softmax_update(
                            e, cur[0], cur[1], lambda e=e, i=i: mask_diagonal(st_ref[e, i % ST_RING], masked))
                        p_ref[e, i % P_RING] = p
                        acc_ref[e, prev[0]] = a_prev[e] * acc_ref[e, prev[0]] + pv_prev
                        new_a.append(alpha)
                    prev, cur, nxt, a_prev = cur, nxt, nxt2, tuple(new_a)
                return prev, cur, nxt, a_prev

            ones = tuple(jnp.ones((1, tq), F32) for _ in heads)
            spare = (jnp.int32(nq), jnp.int32(0))
            prev, cur, _, a_prev = lax.fori_loop(0, n_main // unroll, body, (spare, first, second, ones))
            for e in heads:
                pv_prev = weighted_values(e, prev[1], p_ref[e, (unroll - 1) % P_RING])
                acc_ref[e, prev[0]] = a_prev[e] * acc_ref[e, prev[0]] + pv_prev

        def tail(_, pr):
            simple_step(pr, masked)
            return advance(*pr)

        lax.fori_loop(0, npairs - n_main, tail, cur)

    def next_off_diagonal(qi, kj):
        k2 = kj + 1
        wrap = k2 >= qi
        return jnp.where(wrap, jnp.minimum(qi + 1, nq - 1), qi), jnp.where(wrap, 0, k2)

    def next_diagonal(qi, kj):
        nxt = jnp.minimum(qi + 1, nq - 1)
        return nxt, nxt

    zero = jnp.int32(0)
    sweep((jnp.int32(min(1, nq - 1)), zero), next_off_diagonal, nq * (nq - 1) // 2, False)
    sweep((zero, zero), next_diagonal, nq, True)

    def finish(qi, _):
        accs = [acc_ref[e, qi] for e in heads]
        o = jnp.concatenate([a[0:dh] * (1.0 / a[dh:dh + 1]) for a in accs], axis=0).T
        rows = pl.ds(pl.multiple_of(qi * tq, tq), tq)
        zf = z_ref[0, rows, :].astype(F32)
        o_ref[0, rows, :] = (o * _silu(zf)).astype(BF16)
        return 0

    lax.fori_loop(0, nq, finish, 0)


def _fox_attn(qt, kt, vt, z, cum):
    B, W, S = qt.shape
    H = FOX_HEADS
    dh = FOX_HEAD_DIM
    pair = LANES // dh
    tq = TQ_ATTN
    nq = S // tq
    kern = functools.partial(_fox_attn_kernel, tq=tq, seq=S)
    fm = pl.BlockSpec((1, LANES, S), lambda b, h: (b, h, 0))
    return pl.pallas_call(
        kern,
        grid=(B, H // pair),
        in_specs=[
            fm, fm, fm,
            pl.BlockSpec((1, S, LANES), lambda b, h: (b, 0, h)),
            pl.BlockSpec((1, H, S), lambda b, h: (b, 0, 0)),
        ],
        out_specs=pl.BlockSpec((1, S, LANES), lambda b, h: (b, 0, h)),
        out_shape=jax.ShapeDtypeStruct((B, S, W), BF16),
        scratch_shapes=[
            pltpu.VMEM((pair, LANES, S), BF16),
            pltpu.VMEM((pair, S, LANES), BF16),
            pltpu.VMEM((pair, dh + BF16_ROWS, S), BF16),
            pltpu.VMEM((pair, nq, 1, tq), F32),
            pltpu.VMEM((pair, nq, 1, tq), F32),
            pltpu.VMEM((pair, nq + 1, dh + BF16_ROWS, tq), F32),
            pltpu.VMEM((pair, ST_RING, tq, tq), F32),
            pltpu.VMEM((pair, P_RING, tq, tq), BF16),
        ],
        compiler_params=_params("arbitrary", "arbitrary"),
        name="fox_attn",
    )(qt, kt, vt, z, cum)


def _out_kernel(o_ref, x_ref, mod_ref, w_ref, fnw_ref, y_ref, *, final):
    y = _dot(o_ref[0], w_ref[...])
    xn = x_ref[0] + mod_ref[0, 0][2:3, :] * y
    if final:
        ms = jnp.mean(xn * xn, axis=-1, keepdims=True)
        xn = xn * lax.rsqrt(ms + EPS) * fnw_ref[...]
    y_ref[0] = xn


def _out_proj(o, x, mod, w_out, fnw, final):
    B, S, D = x.shape
    width = o.shape[-1]
    tm = TM_PROJ
    kern = functools.partial(_out_kernel, final=final)
    return pl.pallas_call(
        kern,
        grid=(B, S // tm),
        in_specs=[
            pl.BlockSpec((1, tm, width), lambda b, s: (b, s, 0)),
            pl.BlockSpec((1, tm, D), lambda b, s: (b, s, 0)),
            pl.BlockSpec((1, 1, 3, D), lambda b, s: (b, 0, 0, 0)),
            _resident(w_out.shape),
            _resident((1, D)),
        ],
        out_specs=pl.BlockSpec((1, tm, D), lambda b, s: (b, s, 0)),
        out_shape=jax.ShapeDtypeStruct((B, S, D), F32),
        compiler_params=_params("arbitrary", "arbitrary"),
        name="out_proj",
    )(o, x, mod.reshape(B, 1, 3, D), w_out, fnw.reshape(1, D))


def kernel(x, c, norm_w, ada_w, ada_b, a_w_in, a_conv_w, a_A_log, a_dt_bias, a_norm_w, a_w_out,
           b_w_in, b_f_bias, b_qn_w, b_kn_w, b_w_out, final_norm_w):
    B, S, D = x.shape
    assert D == D_MODEL and S % max(TM_PROJ, TM_GDN_IN, TC_DELTA, TQ_ATTN) == 0
    mods = _ada_mod(c, ada_w, ada_b)
    nh = GDN_V_HEADS
    for i in range(DEPTH):
        j = i // 2
        final = i == DEPTH - 1
        if i % 2 == 0:
            w_in = a_w_in[j]
            w_main = w_in[:, :GDN_CONV_CH + GDN_V_WIDTH].astype(BF16)
            w_b = w_in[:, GDN_CONV_CH + GDN_V_WIDTH:GDN_CONV_CH + GDN_V_WIDTH + nh]
            w_a = w_in[:, GDN_CONV_CH + GDN_V_WIDTH + nh:]
            pad = jnp.zeros((D, LANES - nh), F32)
            w_ba = jnp.concatenate([w_b, pad, w_a, pad], axis=1).astype(BF16)
            w_at = w_a.T.astype(BF16)
            q, k, kt, v, z, bg, gt = _gdn_in(x, mods[i], norm_w[i], w_main, w_ba, w_at, a_conv_w[j],
                                              a_A_log[j], a_dt_bias[j])
            o = _gdn_delta(q, k, kt, v, z, bg, gt, a_norm_w[j])
            w_out = a_w_out[j].astype(BF16)
        else:
            w_in = b_w_in[j]
            w_qkv_t = w_in[:, :3 * FOX_WIDTH].T.astype(BF16)
            w_z = w_in[:, 3 * FOX_WIDTH:4 * FOX_WIDTH].astype(BF16)
            w_ft = w_in[:, 4 * FOX_WIDTH:].T.astype(BF16)
            qw = b_qn_w[j] * (FOX_HEAD_DIM ** -0.5 * LOG2E)
            qt, kt, vt, z, cum = _fox_in(x, mods[i], norm_w[i], w_qkv_t, w_z, w_ft, b_f_bias[j], qw, b_kn_w[j])
            o = _fox_attn(qt, kt, vt, z, cum)
            w_out = b_w_out[j].astype(BF16)
        x = _out_proj(o, x, mods[i], w_out, final_norm_w, final)
    return x
```

```python
import functools

import jax
import jax.numpy as jnp
from jax import lax
from jax.experimental import pallas as pl
from jax.experimental.pallas import tpu as pltpu

F32 = jnp.float32
BF16 = jnp.bfloat16

D_MODEL = 1024
DEPTH = 4
EPS = 1e-6
CHUNK = 64

GDN_QK_HEADS = 8
GDN_V_HEADS = 16
GDN_HEAD_DIM = 128
GDN_QK_WIDTH = GDN_QK_HEADS * GDN_HEAD_DIM
GDN_V_WIDTH = GDN_V_HEADS * GDN_HEAD_DIM
GDN_CONV_CH = 2 * GDN_QK_WIDTH + GDN_V_WIDTH
CONV_WIDTH = 4

FOX_HEADS = 16
FOX_HEAD_DIM = 64
FOX_WIDTH = FOX_HEADS * FOX_HEAD_DIM

LANES = 128
SUBLANES = 8
BF16_ROWS = 16
LOG2E = 1.4426950408889634
VMEM_LIMIT = 48 * 1024 * 1024

TM_PROJ = 512
TM_GDN_IN = 256
TC_DELTA = 256
DELTA_CHUNKS_PER_TRIP = 2
TQ_ATTN = 256
CONV_GROUP = 512
FOX_GROUP = 512
KT_BLOCK = 512
ATTN_UNROLL = 8
ST_RING = 4
P_RING = 2


def _sigmoid(x):
    return 1.0 / (1.0 + jnp.exp(-x))


def _silu(x):
    return x * _sigmoid(x)


def _softplus(x):
    return jnp.maximum(x, 0.0) + jnp.log(1.0 + jnp.exp(-jnp.abs(x)))


def _split3(a):
    hi = a.astype(BF16)
    r = a - hi.astype(F32)
    mid = r.astype(BF16)
    lo = (r - mid.astype(F32)).astype(BF16)
    return hi, mid, lo


def _dot(a, b):
    return jnp.dot(a, b, preferred_element_type=F32)


def _dot_nt(a, b):
    return lax.dot_general(a, b, (((1,), (1,)), ((), ())), preferred_element_type=F32)


def _dot_tn(a, b):
    return lax.dot_general(a, b, (((0,), (0,)), ((), ())), preferred_element_type=F32)


def _modulated_norm(x, mod, nw):
    ms = jnp.mean(x * x, axis=-1, keepdims=True)
    y = x * lax.rsqrt(ms + EPS) * nw
    return y * (1.0 + mod[1:2, :]) + mod[0:1, :]


def _params(*sem):
    return pltpu.CompilerParams(dimension_semantics=sem, vmem_limit_bytes=VMEM_LIMIT)


def _resident(shape):
    nd = len(shape)
    return pl.BlockSpec(shape, lambda *_: (0,) * nd, pipeline_mode=pl.Buffered(1))


def _ada_kernel(c_ref, w_ref, b_ref, o_ref):
    cond = _silu(c_ref[...])
    o_ref[0] = _dot(cond, w_ref[0]) + b_ref[0]


def _ada_mod(c, ada_w, ada_b):
    B, D = c.shape
    depth = ada_w.shape[0]
    out = pl.pallas_call(
        _ada_kernel,
        grid=(depth,),
        in_specs=[
            pl.BlockSpec((B, D), lambda i: (0, 0)),
            pl.BlockSpec((1, D, 3 * D), lambda i: (i, 0, 0)),
            pl.BlockSpec((1, 1, 3 * D), lambda i: (i, 0, 0)),
        ],
        out_specs=pl.BlockSpec((1, B, 3 * D), lambda i: (i, 0, 0)),
        out_shape=jax.ShapeDtypeStruct((depth, B, 3 * D), F32),
        compiler_params=_params("arbitrary"),
        name="ada_mod",
    )(c, ada_w, ada_b.reshape(depth, 1, 3 * D))
    return out.reshape(depth, B, 3, D)


def _gdn_in_kernel(x_ref, mod_ref, nw_ref, w_ref, wba_ref, wbat_ref, cw_ref,
                   alog_ref, dtb_ref, alogc_ref, dtbc_ref,
                   q_ref, k_ref, kt_ref, v_ref, z_ref, bg_ref, gt_ref, cbuf_ref, *, tm):
    s = pl.program_id(1)
    tail = SUBLANES
    nchunk = tm // CHUNK

    @pl.when(s == 0)
    def _():
        cbuf_ref[0:tail, :] = jnp.zeros((tail, GDN_CONV_CH), F32)

    h = _modulated_norm(x_ref[0], mod_ref[0, 0], nw_ref[...])
    hb = h.astype(BF16)

    qscale = GDN_HEAD_DIM ** -0.5
    for c in range(GDN_CONV_CH // CONV_GROUP):
        lo = c * CONV_GROUP
        pre = _dot(hb, w_ref[:, lo:lo + CONV_GROUP])
        cbuf_ref[tail:tail + tm, lo:lo + CONV_GROUP] = pre
        acc = cw_ref[CONV_WIDTH - 1:CONV_WIDTH, lo:lo + CONV_GROUP] * pre
        for j in range(CONV_WIDTH - 1):
            off = tail - (CONV_WIDTH - 1) + j
            acc = acc + cw_ref[j:j + 1, lo:lo + CONV_GROUP] * cbuf_ref[off:off + tm, lo:lo + CONV_GROUP]
        y = _silu(acc)
        if lo < 2 * GDN_QK_WIDTH:
            for hh in range(CONV_GROUP // GDN_HEAD_DIM):
                seg = y[:, hh * GDN_HEAD_DIM:(hh + 1) * GDN_HEAD_DIM]
                inv = lax.rsqrt(jnp.sum(seg * seg, axis=-1, keepdims=True) + EPS)
                col = lo + hh * GDN_HEAD_DIM
                if col < GDN_QK_WIDTH:
                    q_ref[0, :, col:col + GDN_HEAD_DIM] = (seg * inv * qscale).astype(BF16)
                else:
                    col -= GDN_QK_WIDTH
                    kn = seg * inv
                    k_ref[0, :, col:col + GDN_HEAD_DIM] = kn.astype(BF16)
                    knt = kn.T.astype(BF16)
                    for cc in range(nchunk):
                        kt_ref[0, cc, col:col + GDN_HEAD_DIM, :] = knt[:, cc * CHUNK:(cc + 1) * CHUNK]
        else:
            col = lo - 2 * GDN_QK_WIDTH
            v_ref[0, :, col:col + CONV_GROUP] = y.astype(BF16)
    cbuf_ref[0:tail, :] = cbuf_ref[tm:tm + tail, :]

    for c in range(GDN_V_WIDTH // CONV_GROUP):
        lo = c * CONV_GROUP
        z = _dot(hb, w_ref[:, GDN_CONV_CH + lo:GDN_CONV_CH + lo + CONV_GROUP])
        z_ref[0, :, lo:lo + CONV_GROUP] = z.astype(BF16)

    nh = GDN_V_HEADS
    ba = _dot(hb, wba_ref[...])
    beta = _sigmoid(ba[:, 0:nh])
    g = -jnp.exp(alog_ref[...]) * _softplus(ba[:, LANES:LANES + nh] + dtb_ref[...])
    row = lax.broadcasted_iota(jnp.int32, (tm, tm), 0)
    col = lax.broadcasted_iota(jnp.int32, (tm, tm), 1)
    same = (row // CHUNK) == (col // CHUNK)
    tri_lo = jnp.where(same & (col <= row), 1.0, 0.0).astype(BF16)
    tri_up = jnp.where(same & (row <= col), 1.0, 0.0).astype(BF16)
    gc = sum(_dot(tri_lo, p) for p in _split3(g))
    bg_ref[0, :, 0:nh] = beta
    bg_ref[0, :, nh:2 * nh] = gc
    bat = _dot_nt(wbat_ref[...], hb)
    g_t = -jnp.exp(alogc_ref[...]) * _softplus(bat + dtbc_ref[...])
    gct = sum(_dot(p, tri_up) for p in _split3(g_t))
    for cc in range(nchunk):
        gt_ref[0, cc] = gct[:, cc * CHUNK:(cc + 1) * CHUNK]


def _gdn_in(x, mod, nw, w_main, w_ba, w_at, conv_w, a_log, dt_bias):
    B, S, D = x.shape
    tm = TM_GDN_IN
    nh = GDN_V_HEADS
    kern = functools.partial(_gdn_in_kernel, tm=tm)
    tok = lambda width: pl.BlockSpec((1, tm, width), lambda b, s: (b, s, 0))
    return pl.pallas_call(
        kern,
        grid=(B, S // tm),
        in_specs=[
            tok(D),
            pl.BlockSpec((1, 1, 3, D), lambda b, s: (b, 0, 0, 0)),
            _resident((1, D)),
            _resident(w_main.shape),
            _resident(w_ba.shape),
            _resident(w_at.shape),
            _resident(conv_w.shape),
            _resident((1, nh)), _resident((1, nh)), _resident((nh, 1)), _resident((nh, 1)),
        ],
        out_specs=[
            tok(GDN_QK_WIDTH), tok(GDN_QK_WIDTH),
            pl.BlockSpec((1, tm // CHUNK, GDN_QK_WIDTH, CHUNK), lambda b, s: (b, s, 0, 0)),
            tok(GDN_V_WIDTH), tok(GDN_V_WIDTH),
            tok(2 * nh),
            pl.BlockSpec((1, tm // CHUNK, nh, CHUNK), lambda b, s: (b, s, 0, 0)),
        ],
        out_shape=[
            jax.ShapeDtypeStruct((B, S, GDN_QK_WIDTH), BF16),
            jax.ShapeDtypeStruct((B, S, GDN_QK_WIDTH), BF16),
            jax.ShapeDtypeStruct((B, S // CHUNK, GDN_QK_WIDTH, CHUNK), BF16),
            jax.ShapeDtypeStruct((B, S, GDN_V_WIDTH), BF16),
            jax.ShapeDtypeStruct((B, S, GDN_V_WIDTH), BF16),
            jax.ShapeDtypeStruct((B, S, 2 * nh), F32),
            jax.ShapeDtypeStruct((B, S // CHUNK, nh, CHUNK), F32),
        ],
        scratch_shapes=[pltpu.VMEM((tm + SUBLANES, GDN_CONV_CH), F32)],
        compiler_params=_params("arbitrary", "arbitrary"),
        name="gdn_in",
    )(x, mod.reshape(B, 1, 3, D), nw.reshape(1, D), w_main, w_ba, w_at, conv_w,
      a_log.reshape(1, nh), dt_bias.reshape(1, nh), a_log.reshape(nh, 1), dt_bias.reshape(nh, 1))


def _delta_kernel(q_ref, k_ref, kt_ref, v_ref, z_ref, bg_ref, gt_ref, nw_ref, o_ref, s_ref, *, tc):
    s = pl.program_id(1)
    C = CHUNK
    dh = GDN_HEAD_DIM
    nqk = GDN_QK_HEADS
    nv = GDN_V_HEADS
    rep = nv // nqk
    heads = range(nv)

    @pl.when(s == 0)
    def _():
        s_ref[...] = jnp.zeros(s_ref.shape, F32)

    row = lax.broadcasted_iota(jnp.int32, (C, C), 0)
    col = lax.broadcasted_iota(jnp.int32, (C, C), 1)
    lower = row >= col
    strict = row > col
    eye = jnp.where(row == col, 1.0, 0.0).astype(F32)
    nw = nw_ref[...]

    nper = DELTA_CHUNKS_PER_TRIP
    units = [(t, h) for t in range(nper) for h in heads]

    def trip(i, _):
        cs = [i * nper + t for t in range(nper)]
        rows = [pl.ds(pl.multiple_of(c * C, C), C) for c in cs]
        bg = [bg_ref[0, rows[t], :] for t in range(nper)]
        gt = [gt_ref[0, cs[t]] for t in range(nper)]
        groups = [(t, j) for t in range(nper) for j in range(nqk)]
        qb = {(t, j): q_ref[0, rows[t], j * dh:(j + 1) * dh] for t, j in groups}
        kb = {(t, j): k_ref[0, rows[t], j * dh:(j + 1) * dh] for t, j in groups}
        ktb = {(t, j): kt_ref[0, cs[t], j * dh:(j + 1) * dh, :] for t, j in groups}
        kq = {g: _dot(jnp.concatenate([kb[g], qb[g]], axis=0), ktb[g]) for g in groups}
        grp = {(t, h): (t, h // rep) for t, h in units}

        beta = {(t, h): bg[t][:, h:h + 1] for t, h in units}
        gc = {(t, h): bg[t][:, nv + h:nv + h + 1] for t, h in units}
        gr = {(t, h): gt[t][h:h + 1, :] for t, h in units}
        g_last = {u: gr[u][:, C - 1:C] for u in units}
        decay = {u: jnp.exp(jnp.where(lower, gc[u] - gr[u], -jnp.inf)) for u in units}
        egc = {u: jnp.exp(gc[u]) for u in units}

        X = {u: -(jnp.where(strict, kq[grp[u]][:C] * decay[u], 0.0) * beta[u]) for u in units}
        negL = X
        P = {u: eye + X[u] for u in units}
        Xb = {u: X[u].astype(BF16) for u in units}
        X = {u: _dot(Xb[u], Xb[u]) for u in units}
        p = 4
        while p < C:
            Xb = {u: X[u].astype(BF16) for u in units}
            PX = {u: _dot(jnp.concatenate([P[u], X[u]], axis=0).astype(BF16), Xb[u]) for u in units}
            P = {u: P[u] + PX[u][:C] for u in units}
            X = {u: PX[u][C:] for u in units}
            p *= 2
        P = {u: P[u] + _dot(P[u].astype(BF16), X[u].astype(BF16)) for u in units}
        Th, Tl, rest = {}, {}, {}
        for u in units:
            Th[u] = P[u].astype(BF16)
            Tl[u] = (P[u] - Th[u].astype(F32)).astype(BF16)
            nl_hi = negL[u].astype(BF16)
            nl_lo = (negL[u] - nl_hi.astype(F32)).astype(BF16)
            hl = _dot(jnp.concatenate([nl_hi, nl_lo], axis=0), Th[u])
            rest[u] = (eye - P[u]) + (hl[:C] + hl[C:] + _dot(nl_hi, Tl[u]))
        P = {u: P[u] + _dot(Th[u], rest[u].astype(BF16)) for u in units}

        uw = {}
        for t, h in units:
            u = (t, h)
            vf = v_ref[0, rows[t], h * dh:(h + 1) * dh].astype(F32)
            kf = kb[grp[u]].astype(F32)
            rhs = jnp.concatenate([vf * beta[u], kf * (beta[u] * egc[u])], axis=1).astype(BF16)
            uw[u] = _dot(P[u].astype(BF16), rhs)
        wq = {u: jnp.concatenate([uw[u][:, dh:], qb[grp[u]].astype(F32) * egc[u]], axis=0).astype(BF16)
              for u in units}
        ak_lhs = {}
        for u in units:
            attn = (kq[grp[u]][C:] * decay[u]).astype(BF16)
            kdt = (ktb[grp[u]].astype(F32) * jnp.exp(g_last[u] - gr[u])).astype(BF16)
            ak_lhs[u] = jnp.concatenate([attn, kdt], axis=0)
        for t in range(nper):
            S = [s_ref[h] for h in heads]
            r = [_dot(wq[(t, h)], S[h].astype(BF16)) for h in heads]
            vnb = [(uw[(t, h)][:, :dh] - r[h][:C]).astype(BF16) for h in heads]
            ak = [_dot(ak_lhs[(t, h)], vnb[h]) for h in heads]
            for h in heads:
                s_ref[h] = S[h] * jnp.exp(g_last[(t, h)]) + ak[h][C:]
            for h in heads:
                o = r[h][C:] + ak[h][:C]
                var = jnp.mean(o * o, axis=-1, keepdims=True)
                zf = z_ref[0, rows[t], h * dh:(h + 1) * dh].astype(F32)
                o_ref[0, rows[t], h * dh:(h + 1) * dh] = (o * lax.rsqrt(var + EPS) * nw * _silu(zf)).astype(BF16)
        return 0

    lax.fori_loop(0, tc // (C * nper), trip, 0)


def _gdn_delta(q, k, kt, v, z, bg, gt, nw):
    B, S, _ = q.shape
    tc = TC_DELTA
    nv = GDN_V_HEADS
    dh = GDN_HEAD_DIM
    kern = functools.partial(_delta_kernel, tc=tc)
    tok = lambda width: pl.BlockSpec((1, tc, width), lambda b, s: (b, s, 0))
    return pl.pallas_call(
        kern,
        grid=(B, S // tc),
        in_specs=[
            tok(GDN_QK_WIDTH), tok(GDN_QK_WIDTH),
            pl.BlockSpec((1, tc // CHUNK, GDN_QK_WIDTH, CHUNK), lambda b, s: (b, s, 0, 0)),
            tok(GDN_V_WIDTH), tok(GDN_V_WIDTH),
            tok(2 * nv),
            pl.BlockSpec((1, tc // CHUNK, nv, CHUNK), lambda b, s: (b, s, 0, 0)),
            pl.BlockSpec((1, dh), lambda b, s: (0, 0)),
        ],
        out_specs=tok(GDN_V_WIDTH),
        out_shape=jax.ShapeDtypeStruct((B, S, GDN_V_WIDTH), BF16),
        scratch_shapes=[pltpu.VMEM((nv, dh, dh), F32)],
        compiler_params=_params("arbitrary", "arbitrary"),
        name="gdn_delta",
    )(q, k, kt, v, z, bg, gt, nw.reshape(1, dh))


def _fox_in_kernel(x_ref, mod_ref, nw_ref, wt_ref, wz_ref, wft_ref, fb_ref, qw_ref, kw_ref,
                   qt_ref, kt_ref, vt_ref, z_ref, cum_ref, carry_ref, *, tm):
    s = pl.program_id(1)
    W = FOX_WIDTH
    dh = FOX_HEAD_DIM

    @pl.when(s == 0)
    def _():
        carry_ref[...] = jnp.zeros(carry_ref.shape, F32)

    h = _modulated_norm(x_ref[0], mod_ref[0, 0], nw_ref[...])
    hb = h.astype(BF16)

    grp = FOX_GROUP
    for g in range(3 * W // grp):
        blk = _dot_nt(wt_ref[g * grp:(g + 1) * grp, :], hb)
        r0 = (g * grp) % W
        if g * grp < 2 * W:
            is_q = g * grp < W
            wcol = qw_ref[...] if is_q else kw_ref[...]
            dst = qt_ref if is_q else kt_ref
            for hh in range(grp // dh):
                seg = blk[hh * dh:(hh + 1) * dh]
                inv = lax.rsqrt(jnp.mean(seg * seg, axis=0, keepdims=True) + EPS)
                dst[0, r0 + hh * dh:r0 + (hh + 1) * dh, :] = (seg * inv * wcol).astype(BF16)
        else:
            vt_ref[0, r0:r0 + grp, :] = blk.astype(BF16)
    z_ref[0] = _dot(hb, wz_ref[...]).astype(BF16)

    xf = _dot_nt(wft_ref[...], hb) + fb_ref[...]
    log_f = -_softplus(-xf)
    row = lax.broadcasted_iota(jnp.int32, (tm, tm), 0)
    col = lax.broadcasted_iota(jnp.int32, (tm, tm), 1)
    tri_up = jnp.where(row <= col, 1.0, 0.0).astype(BF16)
    cum = sum(_dot(p, tri_up) for p in _split3(log_f)) + carry_ref[...]
    cum_ref[0] = cum
    carry_ref[...] = cum[:, tm - 1:tm]


def _fox_in(x, mod, nw, w_qkv_t, w_z, w_ft, f_bias, qw, kw):
    B, S, D = x.shape
    tm = TM_PROJ
    H = FOX_HEADS
    dh = FOX_HEAD_DIM
    W = FOX_WIDTH
    kern = functools.partial(_fox_in_kernel, tm=tm)
    feat_major = pl.BlockSpec((1, W, tm), lambda b, s: (b, 0, s))
    fm_shape = jax.ShapeDtypeStruct((B, W, S), BF16)
    return pl.pallas_call(
        kern,
        grid=(B, S // tm),
        in_specs=[
            pl.BlockSpec((1, tm, D), lambda b, s: (b, s, 0)),
            pl.BlockSpec((1, 1, 3, D), lambda b, s: (b, 0, 0, 0)),
            _resident((1, D)),
            _resident(w_qkv_t.shape),
            _resident(w_z.shape),
            _resident(w_ft.shape),
            _resident((H, 1)),
            _resident((dh, 1)),
            _resident((dh, 1)),
        ],
        out_specs=[
            feat_major, feat_major, feat_major,
            pl.BlockSpec((1, tm, W), lambda b, s: (b, s, 0)),
            pl.BlockSpec((1, H, tm), lambda b, s: (b, 0, s)),
        ],
        out_shape=[
            fm_shape, fm_shape, fm_shape,
            jax.ShapeDtypeStruct((B, S, W), BF16),
            jax.ShapeDtypeStruct((B, H, S), F32),
        ],
        scratch_shapes=[pltpu.VMEM((H, 1), F32)],
        compiler_params=_params("arbitrary", "arbitrary"),
        name="fox_in",
    )(x, mod.reshape(B, 1, 3, D), nw.reshape(1, D), w_qkv_t, w_z, w_ft, f_bias.reshape(H, 1),
      qw.reshape(dh, 1), kw.reshape(dh, 1))


def _fox_attn_kernel(qt_ref, kt_ref, vt_ref, z_ref, cum_ref, o_ref,
                     qa_ref, ka_ref, va_ref, r_ref, m_ref, acc_ref, st_ref, p_ref, *, tq, seq):
    hp = pl.program_id(1)
    dh = FOX_HEAD_DIM
    pair = LANES // dh
    nq = seq // tq
    nsplit = 3
    heads = range(pair)
    kidx = lax.broadcasted_iota(jnp.int32, (tq, tq), 0)
    qidx = lax.broadcasted_iota(jnp.int32, (tq, tq), 1)
    causal = kidx <= qidx

    aug_row = lax.broadcasted_iota(jnp.int32, (BF16_ROWS, seq), 0)
    pick = jnp.where(aug_row < nsplit, 1.0, 0.0).astype(BF16)
    one_row = jnp.where(aug_row < 1, 1.0, 0.0).astype(BF16)
    for e in heads:
        qa_ref[e, 0:dh, :] = qt_ref[0, e * dh:(e + 1) * dh, :]
        qa_ref[e, dh:dh + BF16_ROWS, :] = pick
        qa_ref[e, dh + BF16_ROWS:, :] = jnp.zeros((LANES - dh - BF16_ROWS, seq), BF16)
        va_ref[e, 0:dh, :] = vt_ref[0, e * dh:(e + 1) * dh, :]
        va_ref[e, dh:, :] = one_row
        crow = cum_ref[0, pl.ds(pair * hp + e, 1), :]
        firsts = [jnp.broadcast_to(crow[:, j * tq:j * tq + 1], (1, tq)) for j in range(nq)]
        for j in range(nq):
            r_ref[e, j] = firsts[j] * (-LOG2E)
        rel = (crow - jnp.concatenate(firsts, axis=1)) * (-LOG2E)
        parts = [p.astype(F32) for p in _split3(rel)]
        btile = jnp.concatenate(parts + [jnp.zeros((SUBLANES - nsplit, seq), F32)], axis=0)
        for blk in range(seq // KT_BLOCK):
            sl = slice(blk * KT_BLOCK, (blk + 1) * KT_BLOCK)
            top = jnp.concatenate([kt_ref[0, e * dh:(e + 1) * dh, sl].astype(F32), btile[:, sl],
                                   jnp.zeros((LANES - dh - SUBLANES, KT_BLOCK), F32)], axis=0)
            ka_ref[e, sl, :] = top.T.astype(BF16)

    m_ref[...] = jnp.full(m_ref.shape, -jnp.inf, F32)
    acc_ref[...] = jnp.zeros(acc_ref.shape, F32)
    p_ref[...] = jnp.zeros(p_ref.shape, BF16)

    def scores(e, qi, kj):
        return _dot(ka_ref[e, pl.ds(pl.multiple_of(kj * tq, tq), tq), :],
                    qa_ref[e, :, pl.ds(pl.multiple_of(qi * tq, tq), tq)])

    def weighted_values(e, kj, p):
        return _dot(va_ref[e, :, pl.ds(pl.multiple_of(kj * tq, tq), tq)], p)

    def softmax_update(e, qi, kj, read_scores):
        off = r_ref[e, kj]
        m_rel = m_ref[e, qi] - off
        m_new = jnp.maximum(m_rel, jnp.max(read_scores(), axis=0, keepdims=True))
        m_ref[e, qi] = m_new + off
        return jnp.exp2(m_rel - m_new), jnp.exp2(read_scores() - m_new).astype(BF16)

    def mask_diagonal(st, masked):
        return jnp.where(causal, st, -jnp.inf) if masked else st

    def simple_step(pr, masked):
        qi, kj = pr
        for e in heads:
            st = mask_diagonal(scores(e, qi, kj), masked)
            alpha, p = softmax_update(e, qi, kj, lambda st=st: st)
            acc_ref[e, qi] = alpha * acc_ref[e, qi] + weighted_values(e, kj, p)

    def sweep(first, advance, npairs, masked):
        unroll = ATTN_UNROLL
        n_main = (npairs // unroll) * unroll
        cur = first
        if n_main:
            second = advance(*first)
            for e in heads:
                st_ref[e, 0] = scores(e, *first)
                st_ref[e, 1] = scores(e, *second)

            def body(_, carry):
                prev, cur, nxt, a_prev = carry
                for i in range(unroll):
                    nxt2 = advance(*nxt)
                    new_a = []
                    for e in heads:
                        pv_prev = weighted_values(e, prev[1], p_ref[e, (i - 1) % P_RING])
                        st_ref[e, (i + 2) % ST_RING] = scores(e, *nxt2)
                        alpha, p = softmax_update(
                            e, cur[0], cur[1], lambda e=e, i=i: mask_diagonal(st_ref[e, i % ST_RING], masked))
                        p_ref[e, i % P_RING] = p
                        acc_ref[e, prev[0]] = a_prev[e] * acc_ref[e, prev[0]] + pv_prev
                        new_a.append(alpha)
                    prev, cur, nxt, a_prev = cur, nxt, nxt2, tuple(new_a)
                return prev, cur, nxt, a_prev

            ones = tuple(jnp.ones((1, tq), F32) for _ in heads)
            spare = (jnp.int32(nq), jnp.int32(0))
            prev, cur, _, a_prev = lax.fori_loop(0, n_main // unroll, body, (spare, first, second, ones))
            for e in heads:
                pv_prev = weighted_values(e, prev[1], p_ref[e, (unroll - 1) % P_RING])
                acc_ref[e, prev[0]] = a_prev[e] * acc_ref[e, prev[0]] + pv_prev

        def tail(_, pr):
            simple_step(pr, masked)
            return advance(*pr)

        lax.fori_loop(0, npairs - n_main, tail, cur)

    def next_off_diagonal(qi, kj):
        k2 = kj + 1
        wrap = k2 >= qi
        return jnp.where(wrap, jnp.minimum(qi + 1, nq - 1), qi), jnp.where(wrap, 0, k2)

    def next_diagonal(qi, kj):
        nxt = jnp.minimum(qi + 1, nq - 1)
        return nxt, nxt

    zero = jnp.int32(0)
    sweep((jnp.int32(min(1, nq - 1)), zero), next_off_diagonal, nq * (nq - 1) // 2, False)
    sweep((zero, zero), next_diagonal, nq, True)

    def finish(qi, _):
        accs = [acc_ref[e, qi] for e in heads]
        o = jnp.concatenate([a[0:dh] * (1.0 / a[dh:dh + 1]) for a in accs], axis=0).T
        rows = pl.ds(pl.multiple_of(qi * tq, tq), tq)
        zf = z_ref[0, rows, :].astype(F32)
        o_ref[0, rows, :] = (o * _silu(zf)).astype(BF16)
        return 0

    lax.fori_loop(0, nq, finish, 0)


def _fox_attn(qt, kt, vt, z, cum):
    B, W, S = qt.shape
    H = FOX_HEADS
    dh = FOX_HEAD_DIM
    pair = LANES // dh
    tq = TQ_ATTN
    nq = S // tq
    kern = functools.partial(_fox_attn_kernel, tq=tq, seq=S)
    fm = pl.BlockSpec((1, LANES, S), lambda b, h: (b, h, 0))
    return pl.pallas_call(
        kern,
        grid=(B, H // pair),
        in_specs=[
            fm, fm, fm,
            pl.BlockSpec((1, S, LANES), lambda b, h: (b, 0, h)),
            pl.BlockSpec((1, H, S), lambda b, h: (b, 0, 0)),
        ],
        out_specs=pl.BlockSpec((1, S, LANES), lambda b, h: (b, 0, h)),
        out_shape=jax.ShapeDtypeStruct((B, S, W), BF16),
        scratch_shapes=[
            pltpu.VMEM((pair, LANES, S), BF16),
            pltpu.VMEM((pair, S, LANES), BF16),
            pltpu.VMEM((pair, dh + BF16_ROWS, S), BF16),
            pltpu.VMEM((pair, nq, 1, tq), F32),
            pltpu.VMEM((pair, nq, 1, tq), F32),
            pltpu.VMEM((pair, nq + 1, dh + BF16_ROWS, tq), F32),
            pltpu.VMEM((pair, ST_RING, tq, tq), F32),
            pltpu.VMEM((pair, P_RING, tq, tq), BF16),
        ],
        compiler_params=_params("arbitrary", "arbitrary"),
        name="fox_attn",
    )(qt, kt, vt, z, cum)


def _out_kernel(o_ref, x_ref, mod_ref, w_ref, fnw_ref, y_ref, *, final):
    y = _dot(o_ref[0], w_ref[...])
    xn = x_ref[0] + mod_ref[0, 0][2:3, :] * y
    if final:
        ms = jnp.mean(xn * xn, axis=-1, keepdims=True)
        xn = xn * lax.rsqrt(ms + EPS) * fnw_ref[...]
    y_ref[0] = xn


def _out_proj(o, x, mod, w_out, fnw, final):
    B, S, D = x.shape
    width = o.shape[-1]
    tm = TM_PROJ
    kern = functools.partial(_out_kernel, final=final)
    return pl.pallas_call(
        kern,
        grid=(B, S // tm),
        in_specs=[
            pl.BlockSpec((1, tm, width), lambda b, s: (b, s, 0)),
            pl.BlockSpec((1, tm, D), lambda b, s: (b, s, 0)),
            pl.BlockSpec((1, 1, 3, D), lambda b, s: (b, 0, 0, 0)),
            _resident(w_out.shape),
            _resident((1, D)),
        ],
        out_specs=pl.BlockSpec((1, tm, D), lambda b, s: (b, s, 0)),
        out_shape=jax.ShapeDtypeStruct((B, S, D), F32),
        compiler_params=_params("arbitrary", "arbitrary"),
        name="out_proj",
    )(o, x, mod.reshape(B, 1, 3, D), w_out, fnw.reshape(1, D))


def kernel(x, c, norm_w, ada_w, ada_b, a_w_in, a_conv_w, a_A_log, a_dt_bias, a_norm_w, a_w_out,
           b_w_in, b_f_bias, b_qn_w, b_kn_w, b_w_out, final_norm_w):
    B, S, D = x.shape
    assert D == D_MODEL and S % max(TM_PROJ, TM_GDN_IN, TC_DELTA, TQ_ATTN) == 0
    mods = _ada_mod(c, ada_w, ada_b)
    nh = GDN_V_HEADS
    for i in range(DEPTH):
        j = i // 2
        final = i == DEPTH - 1
        if i % 2 == 0:
            w_in = a_w_in[j]
            w_main = w_in[:, :GDN_CONV_CH + GDN_V_WIDTH].astype(BF16)
            w_b = w_in[:, GDN_CONV_CH + GDN_V_WIDTH:GDN_CONV_CH + GDN_V_WIDTH + nh]
            w_a = w_in[:, GDN_CONV_CH + GDN_V_WIDTH + nh:]
            pad = jnp.zeros((D, LANES - nh), F32)
            w_ba = jnp.concatenate([w_b, pad, w_a, pad], axis=1).astype(BF16)
            w_at = w_a.T.astype(BF16)
            q, k, kt, v, z, bg, gt = _gdn_in(x, mods[i], norm_w[i], w_main, w_ba, w_at, a_conv_w[j],
                                              a_A_log[j], a_dt_bias[j])
            o = _gdn_delta(q, k, kt, v, z, bg, gt, a_norm_w[j])
            w_out = a_w_out[j].astype(BF16)
        else:
            w_in = b_w_in[j]
            w_qkv_t = w_in[:, :3 * FOX_WIDTH].T.astype(BF16)
            w_z = w_in[:, 3 * FOX_WIDTH:4 * FOX_WIDTH].astype(BF16)
            w_ft = w_in[:, 4 * FOX_WIDTH:].T.astype(BF16)
            qw = b_qn_w[j] * (FOX_HEAD_DIM ** -0.5 * LOG2E)
            qt, kt, vt, z, cum = _fox_in(x, mods[i], norm_w[i], w_qkv_t, w_z, w_ft, b_f_bias[j], qw, b_kn_w[j])
            o = _fox_attn(qt, kt, vt, z, cum)
            w_out = b_w_out[j].astype(BF16)
        x = _out_proj(o, x, mods[i], w_out, final_norm_w, final)
    return x
```

```python
import functools

import jax
import jax.numpy as jnp
from jax import lax
from jax.experimental import pallas as pl
from jax.experimental.pallas import tpu as pltpu

F32 = jnp.float32
BF16 = jnp.bfloat16

D_MODEL = 1024
DEPTH = 4
EPS = 1e-6
CHUNK = 64

GDN_QK_HEADS = 8
GDN_V_HEADS = 16
GDN_HEAD_DIM = 128
GDN_QK_WIDTH = GDN_QK_HEADS * GDN_HEAD_DIM
GDN_V_WIDTH = GDN_V_HEADS * GDN_HEAD_DIM
GDN_CONV_CH = 2 * GDN_QK_WIDTH + GDN_V_WIDTH
CONV_WIDTH = 4

FOX_HEADS = 16
FOX_HEAD_DIM = 64
FOX_WIDTH = FOX_HEADS * FOX_HEAD_DIM

LANES = 128
SUBLANES = 8
BF16_ROWS = 16
LOG2E = 1.4426950408889634
VMEM_LIMIT = 48 * 1024 * 1024

TM_PROJ = 512
TM_GDN_IN = 256
TC_DELTA = 256
DELTA_CHUNKS_PER_TRIP = 2
TQ_ATTN = 256
CONV_GROUP = 512
FOX_GROUP = 512
KT_BLOCK = 512
ATTN_UNROLL = 8
ST_RING = 4
P_RING = 2


def _sigmoid(x):
    return 1.0 / (1.0 + jnp.exp(-x))


def _silu(x):
    return x * _sigmoid(x)


def _softplus(x):
    return jnp.maximum(x, 0.0) + jnp.log(1.0 + jnp.exp(-jnp.abs(x)))


def _split3(a):
    hi = a.astype(BF16)
    r = a - hi.astype(F32)
    mid = r.astype(BF16)
    lo = (r - mid.astype(F32)).astype(BF16)
    return hi, mid, lo


def _dot(a, b):
    return jnp.dot(a, b, preferred_element_type=F32)


def _dot_nt(a, b):
    return lax.dot_general(a, b, (((1,), (1,)), ((), ())), preferred_element_type=F32)


def _dot_tn(a, b):
    return lax.dot_general(a, b, (((0,), (0,)), ((), ())), preferred_element_type=F32)


def _modulated_norm(x, mod, nw):
    ms = jnp.mean(x * x, axis=-1, keepdims=True)
    y = x * lax.rsqrt(ms + EPS) * nw
    return y * (1.0 + mod[1:2, :]) + mod[0:1, :]


def _params(*sem):
    return pltpu.CompilerParams(dimension_semantics=sem, vmem_limit_bytes=VMEM_LIMIT)


def _resident(shape):
    nd = len(shape)
    return pl.BlockSpec(shape, lambda *_: (0,) * nd, pipeline_mode=pl.Buffered(1))


def _ada_kernel(c_ref, w_ref, b_ref, o_ref):
    cond = _silu(c_ref[...])
    o_ref[0] = _dot(cond, w_ref[0]) + b_ref[0]


def _ada_mod(c, ada_w, ada_b):
    B, D = c.shape
    depth = ada_w.shape[0]
    out = pl.pallas_call(
        _ada_kernel,
        grid=(depth,),
        in_specs=[
            pl.BlockSpec((B, D), lambda i: (0, 0)),
            pl.BlockSpec((1, D, 3 * D), lambda i: (i, 0, 0)),
            pl.BlockSpec((1, 1, 3 * D), lambda i: (i, 0, 0)),
        ],
        out_specs=pl.BlockSpec((1, B, 3 * D), lambda i: (i, 0, 0)),
        out_shape=jax.ShapeDtypeStruct((depth, B, 3 * D), F32),
        compiler_params=_params("arbitrary"),
        name="ada_mod",
    )(c, ada_w, ada_b.reshape(depth, 1, 3 * D))
    return out.reshape(depth, B, 3, D)


def _gdn_in_kernel(x_ref, mod_ref, nw_ref, w_ref, wba_ref, wbat_ref, cw_ref,
                   alog_ref, dtb_ref, alogc_ref, dtbc_ref,
                   q_ref, k_ref, kt_ref, v_ref, z_ref, bg_ref, gt_ref, cbuf_ref, *, tm):
    s = pl.program_id(1)
    tail = SUBLANES
    nchunk = tm // CHUNK

    @pl.when(s == 0)
    def _():
        cbuf_ref[0:tail, :] = jnp.zeros((tail, GDN_CONV_CH), F32)

    h = _modulated_norm(x_ref[0], mod_ref[0, 0], nw_ref[...])
    hb = h.astype(BF16)

    qscale = GDN_HEAD_DIM ** -0.5
    for c in range(GDN_CONV_CH // CONV_GROUP):
        lo = c * CONV_GROUP
        pre = _dot(hb, w_ref[:, lo:lo + CONV_GROUP])
        cbuf_ref[tail:tail + tm, lo:lo + CONV_GROUP] = pre
        acc = cw_ref[CONV_WIDTH - 1:CONV_WIDTH, lo:lo + CONV_GROUP] * pre
        for j in range(CONV_WIDTH - 1):
            off = tail - (CONV_WIDTH - 1) + j
            acc = acc + cw_ref[j:j + 1, lo:lo + CONV_GROUP] * cbuf_ref[off:off + tm, lo:lo + CONV_GROUP]
        y = _silu(acc)
        if lo < 2 * GDN_QK_WIDTH:
            for hh in range(CONV_GROUP // GDN_HEAD_DIM):
                seg = y[:, hh * GDN_HEAD_DIM:(hh + 1) * GDN_HEAD_DIM]
                inv = lax.rsqrt(jnp.sum(seg * seg, axis=-1, keepdims=True) + EPS)
                col = lo + hh * GDN_HEAD_DIM
                if col < GDN_QK_WIDTH:
                    q_ref[0, :, col:col + GDN_HEAD_DIM] = (seg * inv * qscale).astype(BF16)
                else:
                    col -= GDN_QK_WIDTH
                    kn = seg * inv
                    k_ref[0, :, col:col + GDN_HEAD_DIM] = kn.astype(BF16)
                    knt = kn.T.astype(BF16)
                    for cc in range(nchunk):
                        kt_ref[0, cc, col:col + GDN_HEAD_DIM, :] = knt[:, cc * CHUNK:(cc + 1) * CHUNK]
        else:
            col = lo - 2 * GDN_QK_WIDTH
            v_ref[0, :, col:col + CONV_GROUP] = y.astype(BF16)
    cbuf_ref[0:tail, :] = cbuf_ref[tm:tm + tail, :]

    for c in range(GDN_V_WIDTH // CONV_GROUP):
        lo = c * CONV_GROUP
        z = _dot(hb, w_ref[:, GDN_CONV_CH + lo:GDN_CONV_CH + lo + CONV_GROUP])
        z_ref[0, :, lo:lo + CONV_GROUP] = z.astype(BF16)

    nh = GDN_V_HEADS
    ba = _dot(hb, wba_ref[...])
    beta = _sigmoid(ba[:, 0:nh])
    g = -jnp.exp(alog_ref[...]) * _softplus(ba[:, LANES:LANES + nh] + dtb_ref[...])
    row = lax.broadcasted_iota(jnp.int32, (tm, tm), 0)
    col = lax.broadcasted_iota(jnp.int32, (tm, tm), 1)
    same = (row // CHUNK) == (col // CHUNK)
    tri_lo = jnp.where(same & (col <= row), 1.0, 0.0).astype(BF16)
    tri_up = jnp.where(same & (row <= col), 1.0, 0.0).astype(BF16)
    gc = sum(_dot(tri_lo, p) for p in _split3(g))
    bg_ref[0, :, 0:nh] = beta
    bg_ref[0, :, nh:2 * nh] = gc
    bat = _dot_nt(wbat_ref[...], hb)
    g_t = -jnp.exp(alogc_ref[...]) * _softplus(bat + dtbc_ref[...])
    gct = sum(_dot(p, tri_up) for p in _split3(g_t))
    for cc in range(nchunk):
        gt_ref[0, cc] = gct[:, cc * CHUNK:(cc + 1) * CHUNK]


def _gdn_in(x, mod, nw, w_main, w_ba, w_at, conv_w, a_log, dt_bias):
    B, S, D = x.shape
    tm = TM_GDN_IN
    nh = GDN_V_HEADS
    kern = functools.partial(_gdn_in_kernel, tm=tm)
    tok = lambda width: pl.BlockSpec((1, tm, width), lambda b, s: (b, s, 0))
    return pl.pallas_call(
        kern,
        grid=(B, S // tm),
        in_specs=[
            tok(D),
            pl.BlockSpec((1, 1, 3, D), lambda b, s: (b, 0, 0, 0)),
            _resident((1, D)),
            _resident(w_main.shape),
            _resident(w_ba.shape),
            _resident(w_at.shape),
            _resident(conv_w.shape),
            _resident((1, nh)), _resident((1, nh)), _resident((nh, 1)), _resident((nh, 1)),
        ],
        out_specs=[
            tok(GDN_QK_WIDTH), tok(GDN_QK_WIDTH),
            pl.BlockSpec((1, tm // CHUNK, GDN_QK_WIDTH, CHUNK), lambda b, s: (b, s, 0, 0)),
            tok(GDN_V_WIDTH), tok(GDN_V_WIDTH),
            tok(2 * nh),
            pl.BlockSpec((1, tm // CHUNK, nh, CHUNK), lambda b, s: (b, s, 0, 0)),
        ],
        out_shape=[
            jax.ShapeDtypeStruct((B, S, GDN_QK_WIDTH), BF16),
            jax.ShapeDtypeStruct((B, S, GDN_QK_WIDTH), BF16),
            jax.ShapeDtypeStruct((B, S // CHUNK, GDN_QK_WIDTH, CHUNK), BF16),
            jax.ShapeDtypeStruct((B, S, GDN_V_WIDTH), BF16),
            jax.ShapeDtypeStruct((B, S, GDN_V_WIDTH), BF16),
            jax.ShapeDtypeStruct((B, S, 2 * nh), F32),
            jax.ShapeDtypeStruct((B, S // CHUNK, nh, CHUNK), F32),
        ],
        scratch_shapes=[pltpu.VMEM((tm + SUBLANES, GDN_CONV_CH), F32)],
        compiler_params=_params("arbitrary", "arbitrary"),
        name="gdn_in",
    )(x, mod.reshape(B, 1, 3, D), nw.reshape(1, D), w_main, w_ba, w_at, conv_w,
      a_log.reshape(1, nh), dt_bias.reshape(1, nh), a_log.reshape(nh, 1), dt_bias.reshape(nh, 1))


def _delta_kernel(q_ref, k_ref, kt_ref, v_ref, z_ref, bg_ref, gt_ref, nw_ref, o_ref, s_ref, *, tc):
    s = pl.program_id(1)
    C = CHUNK
    dh = GDN_HEAD_DIM
    nqk = GDN_QK_HEADS
    nv = GDN_V_HEADS
    rep = nv // nqk
    heads = range(nv)

    @pl.when(s == 0)
    def _():
        s_ref[...] = jnp.zeros(s_ref.shape, F32)

    row = lax.broadcasted_iota(jnp.int32, (C, C), 0)
    col = lax.broadcasted_iota(jnp.int32, (C, C), 1)
    lower = row >= col
    strict = row > col
    eye = jnp.where(row == col, 1.0, 0.0).astype(F32)
    nw = nw_ref[...]

    nper = DELTA_CHUNKS_PER_TRIP
    units = [(t, h) for t in range(nper) for h in heads]

    def trip(i, _):
        cs = [i * nper + t for t in range(nper)]
        rows = [pl.ds(pl.multiple_of(c * C, C), C) for c in cs]
        bg = [bg_ref[0, rows[t], :] for t in range(nper)]
        gt = [gt_ref[0, cs[t]] for t in range(nper)]
        groups = [(t, j) for t in range(nper) for j in range(nqk)]
        qb = {(t, j): q_ref[0, rows[t], j * dh:(j + 1) * dh] for t, j in groups}
        kb = {(t, j): k_ref[0, rows[t], j * dh:(j + 1) * dh] for t, j in groups}
        ktb = {(t, j): kt_ref[0, cs[t], j * dh:(j + 1) * dh, :] for t, j in groups}
        kq = {g: _dot(jnp.concatenate([kb[g], qb[g]], axis=0), ktb[g]) for g in groups}
        grp = {(t, h): (t, h // rep) for t, h in units}

        beta = {(t, h): bg[t][:, h:h + 1] for t, h in units}
        gc = {(t, h): bg[t][:, nv + h:nv + h + 1] for t, h in units}
        gr = {(t, h): gt[t][h:h + 1, :] for t, h in units}
        g_last = {u: gr[u][:, C - 1:C] for u in units}
        decay = {u: jnp.exp(jnp.where(lower, gc[u] - gr[u], -jnp.inf)) for u in units}
        egc = {u: jnp.exp(gc[u]) for u in units}

        X = {u: -(jnp.where(strict, kq[grp[u]][:C] * decay[u], 0.0) * beta[u]) for u in units}
        negL = X
        P = {u: eye + X[u] for u in units}
        Xb = {u: X[u].astype(BF16) for u in units}
        X = {u: _dot(Xb[u], Xb[u]) for u in units}
        p = 4
        while p < C // 2:
            Xb = {u: X[u].astype(BF16) for u in units}
            PX = {u: _dot(jnp.concatenate([P[u], X[u]], axis=0).astype(BF16), Xb[u]) for u in units}
            P = {u: P[u] + PX[u][:C] for u in units}
            X = {u: PX[u][C:] for u in units}
            p *= 2
        P = {u: P[u] + _dot(P[u].astype(BF16), X[u].astype(BF16)) for u in units}
        Th, Tl, rest = {}, {}, {}
        for u in units:
            Th[u] = P[u].astype(BF16)
            Tl[u] = (P[u] - Th[u].astype(F32)).astype(BF16)
            nl_hi = negL[u].astype(BF16)
            nl_lo = (negL[u] - nl_hi.astype(F32)).astype(BF16)
            hl = _dot(jnp.concatenate([nl_hi, nl_lo], axis=0), Th[u])
            rest[u] = (eye - P[u]) + (hl[:C] + hl[C:] + _dot(nl_hi, Tl[u]))
        P = {u: P[u] + _dot(Th[u], rest[u].astype(BF16)) for u in units}

        uw = {}
        for t, h in units:
            u = (t, h)
            vf = v_ref[0, rows[t], h * dh:(h + 1) * dh].astype(F32)
            kf = kb[grp[u]].astype(F32)
            rhs = jnp.concatenate([vf * beta[u], kf * (beta[u] * egc[u])], axis=1).astype(BF16)
            uw[u] = _dot(P[u].astype(BF16), rhs)
        wq = {u: jnp.concatenate([uw[u][:, dh:], qb[grp[u]].astype(F32) * egc[u]], axis=0).astype(BF16)
              for u in units}
        ak_lhs = {}
        for u in units:
            attn = (kq[grp[u]][C:] * decay[u]).astype(BF16)
            kdt = (ktb[grp[u]].astype(F32) * jnp.exp(g_last[u] - gr[u])).astype(BF16)
            ak_lhs[u] = jnp.concatenate([attn, kdt], axis=0)
        for t in range(nper):
            S = [s_ref[h] for h in heads]
            r = [_dot(wq[(t, h)], S[h].astype(BF16)) for h in heads]
            vnb = [(uw[(t, h)][:, :dh] - r[h][:C]).astype(BF16) for h in heads]
            ak = [_dot(ak_lhs[(t, h)], vnb[h]) for h in heads]
            for h in heads:
                s_ref[h] = S[h] * jnp.exp(g_last[(t, h)]) + ak[h][C:]
            for h in heads:
                o = r[h][C:] + ak[h][:C]
                var = jnp.mean(o * o, axis=-1, keepdims=True)
                zf = z_ref[0, rows[t], h * dh:(h + 1) * dh].astype(F32)
                o_ref[0, rows[t], h * dh:(h + 1) * dh] = (o * lax.rsqrt(var + EPS) * nw * _silu(zf)).astype(BF16)
        return 0

    lax.fori_loop(0, tc // (C * nper), trip, 0)


def _gdn_delta(q, k, kt, v, z, bg, gt, nw):
    B, S, _ = q.shape
    tc = TC_DELTA
    nv = GDN_V_HEADS
    dh = GDN_HEAD_DIM
    kern = functools.partial(_delta_kernel, tc=tc)
    tok = lambda width: pl.BlockSpec((1, tc, width), lambda b, s: (b, s, 0))
    return pl.pallas_call(
        kern,
        grid=(B, S // tc),
        in_specs=[
            tok(GDN_QK_WIDTH), tok(GDN_QK_WIDTH),
            pl.BlockSpec((1, tc // CHUNK, GDN_QK_WIDTH, CHUNK), lambda b, s: (b, s, 0, 0)),
            tok(GDN_V_WIDTH), tok(GDN_V_WIDTH),
            tok(2 * nv),
            pl.BlockSpec((1, tc // CHUNK, nv, CHUNK), lambda b, s: (b, s, 0, 0)),
            pl.BlockSpec((1, dh), lambda b, s: (0, 0)),
        ],
        out_specs=tok(GDN_V_WIDTH),
        out_shape=jax.ShapeDtypeStruct((B, S, GDN_V_WIDTH), BF16),
        scratch_shapes=[pltpu.VMEM((nv, dh, dh), F32)],
        compiler_params=_params("arbitrary", "arbitrary"),
        name="gdn_delta",
    )(q, k, kt, v, z, bg, gt, nw.reshape(1, dh))


def _fox_in_kernel(x_ref, mod_ref, nw_ref, wt_ref, wz_ref, wft_ref, fb_ref, qw_ref, kw_ref,
                   qt_ref, kt_ref, vt_ref, z_ref, cum_ref, carry_ref, *, tm):
    s = pl.program_id(1)
    W = FOX_WIDTH
    dh = FOX_HEAD_DIM

    @pl.when(s == 0)
    def _():
        carry_ref[...] = jnp.zeros(carry_ref.shape, F32)

    h = _modulated_norm(x_ref[0], mod_ref[0, 0], nw_ref[...])
    hb = h.astype(BF16)

    grp = FOX_GROUP
    for g in range(3 * W // grp):
        blk = _dot_nt(wt_ref[g * grp:(g + 1) * grp, :], hb)
        r0 = (g * grp) % W
        if g * grp < 2 * W:
            is_q = g * grp < W
            wcol = qw_ref[...] if is_q else kw_ref[...]
            dst = qt_ref if is_q else kt_ref
            for hh in range(grp // dh):
                seg = blk[hh * dh:(hh + 1) * dh]
                inv = lax.rsqrt(jnp.mean(seg * seg, axis=0, keepdims=True) + EPS)
                dst[0, r0 + hh * dh:r0 + (hh + 1) * dh, :] = (seg * inv * wcol).astype(BF16)
        else:
            vt_ref[0, r0:r0 + grp, :] = blk.astype(BF16)
    z_ref[0] = _dot(hb, wz_ref[...]).astype(BF16)

    xf = _dot_nt(wft_ref[...], hb) + fb_ref[...]
    log_f = -_softplus(-xf)
    row = lax.broadcasted_iota(jnp.int32, (tm, tm), 0)
    col = lax.broadcasted_iota(jnp.int32, (tm, tm), 1)
    tri_up = jnp.where(row <= col, 1.0, 0.0).astype(BF16)
    cum = sum(_dot(p, tri_up) for p in _split3(log_f)) + carry_ref[...]
    cum_ref[0] = cum
    carry_ref[...] = cum[:, tm - 1:tm]


def _fox_in(x, mod, nw, w_qkv_t, w_z, w_ft, f_bias, qw, kw):
    B, S, D = x.shape
    tm = TM_PROJ
    H = FOX_HEADS
    dh = FOX_HEAD_DIM
    W = FOX_WIDTH
    kern = functools.partial(_fox_in_kernel, tm=tm)
    feat_major = pl.BlockSpec((1, W, tm), lambda b, s: (b, 0, s))
    fm_shape = jax.ShapeDtypeStruct((B, W, S), BF16)
    return pl.pallas_call(
        kern,
        grid=(B, S // tm),
        in_specs=[
            pl.BlockSpec((1, tm, D), lambda b, s: (b, s, 0)),
            pl.BlockSpec((1, 1, 3, D), lambda b, s: (b, 0, 0, 0)),
            _resident((1, D)),
            _resident(w_qkv_t.shape),
            _resident(w_z.shape),
            _resident(w_ft.shape),
            _resident((H, 1)),
            _resident((dh, 1)),
            _resident((dh, 1)),
        ],
        out_specs=[
            feat_major, feat_major, feat_major,
            pl.BlockSpec((1, tm, W), lambda b, s: (b, s, 0)),
            pl.BlockSpec((1, H, tm), lambda b, s: (b, 0, s)),
        ],
        out_shape=[
            fm_shape, fm_shape, fm_shape,
            jax.ShapeDtypeStruct((B, S, W), BF16),
            jax.ShapeDtypeStruct((B, H, S), F32),
        ],
        scratch_shapes=[pltpu.VMEM((H, 1), F32)],
        compiler_params=_params("arbitrary", "arbitrary"),
        name="fox_in",
    )(x, mod.reshape(B, 1, 3, D), nw.reshape(1, D), w_qkv_t, w_z, w_ft, f_bias.reshape(H, 1),
      qw.reshape(dh, 1), kw.reshape(dh, 1))


def _fox_attn_kernel(qt_ref, kt_ref, vt_ref, z_ref, cum_ref, o_ref,
                     qa_ref, ka_ref, va_ref, r_ref, m_ref, acc_ref, st_ref, p_ref, *, tq, seq):
    hp = pl.program_id(1)
    dh = FOX_HEAD_DIM
    pair = LANES // dh
    nq = seq // tq
    nsplit = 3
    heads = range(pair)
    kidx = lax.broadcasted_iota(jnp.int32, (tq, tq), 0)
    qidx = lax.broadcasted_iota(jnp.int32, (tq, tq), 1)
    causal = kidx <= qidx

    aug_row = lax.broadcasted_iota(jnp.int32, (BF16_ROWS, seq), 0)
    pick = jnp.where(aug_row < nsplit, 1.0, 0.0).astype(BF16)
    one_row = jnp.where(aug_row < 1, 1.0, 0.0).astype(BF16)
    for e in heads:
        qa_ref[e, 0:dh, :] = qt_ref[0, e * dh:(e + 1) * dh, :]
        qa_ref[e, dh:dh + BF16_ROWS, :] = pick
        qa_ref[e, dh + BF16_ROWS:, :] = jnp.zeros((LANES - dh - BF16_ROWS, seq), BF16)
        va_ref[e, 0:dh, :] = vt_ref[0, e * dh:(e + 1) * dh, :]
        va_ref[e, dh:, :] = one_row
        crow = cum_ref[0, pl.ds(pair * hp + e, 1), :]
        firsts = [jnp.broadcast_to(crow[:, j * tq:j * tq + 1], (1, tq)) for j in range(nq)]
        for j in range(nq):
            r_ref[e, j] = firsts[j] * (-LOG2E)
        rel = (crow - jnp.concatenate(firsts, axis=1)) * (-LOG2E)
        parts = [p.astype(F32) for p in _split3(rel)]
        btile = jnp.concatenate(parts + [jnp.zeros((SUBLANES - nsplit, seq), F32)], axis=0)
        for blk in range(seq // KT_BLOCK):
            sl = slice(blk * KT_BLOCK, (blk + 1) * KT_BLOCK)
            top = jnp.concatenate([kt_ref[0, e * dh:(e + 1) * dh, sl].astype(F32), btile[:, sl],
                                   jnp.zeros((LANES - dh - SUBLANES, KT_BLOCK), F32)], axis=0)
            ka_ref[e, sl, :] = top.T.astype(BF16)

    m_ref[...] = jnp.full(m_ref.shape, -jnp.inf, F32)
    acc_ref[...] = jnp.zeros(acc_ref.shape, F32)
    p_ref[...] = jnp.zeros(p_ref.shape, BF16)

    def scores(e, qi, kj):
        return _dot(ka_ref[e, pl.ds(pl.multiple_of(kj * tq, tq), tq), :],
                    qa_ref[e, :, pl.ds(pl.multiple_of(qi * tq, tq), tq)])

    def weighted_values(e, kj, p):
        return _dot(va_ref[e, :, pl.ds(pl.multiple_of(kj * tq, tq), tq)], p)

    def softmax_update(e, qi, kj, read_scores):
        off = r_ref[e, kj]
        m_rel = m_ref[e, qi] - off
        m_new = jnp.maximum(m_rel, jnp.max(read_scores(), axis=0, keepdims=True))
        m_ref[e, qi] = m_new + off
        return jnp.exp2(m_rel - m_new), jnp.exp2(read_scores() - m_new).astype(BF16)

    def mask_diagonal(st, masked):
        return jnp.where(causal, st, -jnp.inf) if masked else st

    def simple_step(pr, masked):
        qi, kj = pr
        for e in heads:
            st = mask_diagonal(scores(e, qi, kj), masked)
            alpha, p = softmax_update(e, qi, kj, lambda st=st: st)
            acc_ref[e, qi] = alpha * acc_ref[e, qi] + weighted_values(e, kj, p)

    def sweep(first, advance, npairs, masked):
        unroll = ATTN_UNROLL
        n_main = (npairs // unroll) * unroll
        cur = first
        if n_main:
            second = advance(*first)
            for e in heads:
                st_ref[e, 0] = scores(e, *first)
                st_ref[e, 1] = scores(e, *second)

            def body(_, carry):
                prev, cur, nxt, a_prev = carry
                for i in range(unroll):
                    nxt2 = advance(*nxt)
                    new_a = []
                    for e in heads:
                        pv_prev = weighted_values(e, prev[1], p_ref[e, (i - 1) % P_RING])
                        st_ref[e, (i + 2) % ST_RING] = scores(e, *nxt2)
                        alpha, p = softmax_update(
                            e, cur[0], cur[1], lambda e=e, i=i: mask_diagonal(st_ref[e, i % ST_RING], masked))
                        p_ref[e, i % P_RING] = p
                        acc_ref[e, prev[0]] = a_prev[e] * acc_ref[e, prev[0]] + pv_prev
                        new_a.append(alpha)
                    prev, cur, nxt, a_prev = cur, nxt, nxt2, tuple(new_a)
                return prev, cur, nxt, a_prev

            ones = tuple(jnp.ones((1, tq), F32) for _ in heads)
            spare = (jnp.int32(nq), jnp.int32(0))
            prev, cur, _, a_prev = lax.fori_loop(0, n_main // unroll, body, (spare, first, second, ones))
            for e in heads:
                pv_prev = weighted_values(e, prev[1], p_ref[e, (unroll - 1) % P_RING])
                acc_ref[e, prev[0]] = a_prev[e] * acc_ref[e, prev[0]] + pv_prev

        def tail(_, pr):
            simple_step(pr, masked)
            return advance(*pr)

        lax.fori_loop(0, npairs - n_main, tail, cur)

    def next_off_diagonal(qi, kj):
        k2 = kj + 1
        wrap = k2 >= qi
        return jnp.where(wrap, jnp.minimum(qi + 1, nq - 1), qi), jnp.where(wrap, 0, k2)

    def next_diagonal(qi, kj):
        nxt = jnp.minimum(qi + 1, nq - 1)
        return nxt, nxt

    zero = jnp.int32(0)
    sweep((jnp.int32(min(1, nq - 1)), zero), next_off_diagonal, nq * (nq - 1) // 2, False)
    sweep((zero, zero), next_diagonal, nq, True)

    def finish(qi, _):
        accs = [acc_ref[e, qi] for e in heads]
        o = jnp.concatenate([a[0:dh] * (1.0 / a[dh:dh + 1]) for a in accs], axis=0).T
        rows = pl.ds(pl.multiple_of(qi * tq, tq), tq)
        zf = z_ref[0, rows, :].astype(F32)
        o_ref[0, rows, :] = (o * _silu(zf)).astype(BF16)
        return 0

    lax.fori_loop(0, nq, finish, 0)


def _fox_attn(qt, kt, vt, z, cum):
    B, W, S = qt.shape
    H = FOX_HEADS
    dh = FOX_HEAD_DIM
    pair = LANES // dh
    tq = TQ_ATTN
    nq = S // tq
    kern = functools.partial(_fox_attn_kernel, tq=tq, seq=S)
    fm = pl.BlockSpec((1, LANES, S), lambda b, h: (b, h, 0))
    return pl.pallas_call(
        kern,
        grid=(B, H // pair),
        in_specs=[
            fm, fm, fm,
            pl.BlockSpec((1, S, LANES), lambda b, h: (b, 0, h)),
            pl.BlockSpec((1, H, S), lambda b, h: (b, 0, 0)),
        ],
        out_specs=pl.BlockSpec((1, S, LANES), lambda b, h: (b, 0, h)),
        out_shape=jax.ShapeDtypeStruct((B, S, W), BF16),
        scratch_shapes=[
            pltpu.VMEM((pair, LANES, S), BF16),
            pltpu.VMEM((pair, S, LANES), BF16),
            pltpu.VMEM((pair, dh + BF16_ROWS, S), BF16),
            pltpu.VMEM((pair, nq, 1, tq), F32),
            pltpu.VMEM((pair, nq, 1, tq), F32),
            pltpu.VMEM((pair, nq + 1, dh + BF16_ROWS, tq), F32),
            pltpu.VMEM((pair, ST_RING, tq, tq), F32),
            pltpu.VMEM((pair, P_RING, tq, tq), BF16),
        ],
        compiler_params=_params("arbitrary", "arbitrary"),
        name="fox_attn",
    )(qt, kt, vt, z, cum)


def _out_kernel(o_ref, x_ref, mod_ref, w_ref, fnw_ref, y_ref, *, final):
    y = _dot(o_ref[0], w_ref[...])
    xn = x_ref[0] + mod_ref[0, 0][2:3, :] * y
    if final:
        ms = jnp.mean(xn * xn, axis=-1, keepdims=True)
        xn = xn * lax.rsqrt(ms + EPS) * fnw_ref[...]
    y_ref[0] = xn


def _out_proj(o, x, mod, w_out, fnw, final):
    B, S, D = x.shape
    width = o.shape[-1]
    tm = TM_PROJ
    kern = functools.partial(_out_kernel, final=final)
    return pl.pallas_call(
        kern,
        grid=(B, S // tm),
        in_specs=[
            pl.BlockSpec((1, tm, width), lambda b, s: (b, s, 0)),
            pl.BlockSpec((1, tm, D), lambda b, s: (b, s, 0)),
            pl.BlockSpec((1, 1, 3, D), lambda b, s: (b, 0, 0, 0)),
            _resident(w_out.shape),
            _resident((1, D)),
        ],
        out_specs=pl.BlockSpec((1, tm, D), lambda b, s: (b, s, 0)),
        out_shape=jax.ShapeDtypeStruct((B, S, D), F32),
        compiler_params=_params("arbitrary", "arbitrary"),
        name="out_proj",
    )(o, x, mod.reshape(B, 1, 3, D), w_out, fnw.reshape(1, D))


def kernel(x, c, norm_w, ada_w, ada_b, a_w_in, a_conv_w, a_A_log, a_dt_bias, a_norm_w, a_w_out,
           b_w_in, b_f_bias, b_qn_w, b_kn_w, b_w_out, final_norm_w):
    B, S, D = x.shape
    assert D == D_MODEL and S % max(TM_PROJ, TM_GDN_IN, TC_DELTA, TQ_ATTN) == 0
    mods = _ada_mod(c, ada_w, ada_b)
    nh = GDN_V_HEADS
    for i in range(DEPTH):
        j = i // 2
        final = i == DEPTH - 1
        if i % 2 == 0:
            w_in = a_w_in[j]
            w_main = w_in[:, :GDN_CONV_CH + GDN_V_WIDTH].astype(BF16)
            w_b = w_in[:, GDN_CONV_CH + GDN_V_WIDTH:GDN_CONV_CH + GDN_V_WIDTH + nh]
            w_a = w_in[:, GDN_CONV_CH + GDN_V_WIDTH + nh:]
            pad = jnp.zeros((D, LANES - nh), F32)
            w_ba = jnp.concatenate([w_b, pad, w_a, pad], axis=1).astype(BF16)
            w_at = w_a.T.astype(BF16)
            q, k, kt, v, z, bg, gt = _gdn_in(x, mods[i], norm_w[i], w_main, w_ba, w_at, a_conv_w[j],
                                              a_A_log[j], a_dt_bias[j])
            o = _gdn_delta(q, k, kt, v, z, bg, gt, a_norm_w[j])
            w_out = a_w_out[j].astype(BF16)
        else:
            w_in = b_w_in[j]
            w_qkv_t = w_in[:, :3 * FOX_WIDTH].T.astype(BF16)
            w_z = w_in[:, 3 * FOX_WIDTH:4 * FOX_WIDTH].astype(BF16)
            w_ft = w_in[:, 4 * FOX_WIDTH:].T.astype(BF16)
            qw = b_qn_w[j] * (FOX_HEAD_DIM ** -0.5 * LOG2E)
            qt, kt, vt, z, cum = _fox_in(x, mods[i], norm_w[i], w_qkv_t, w_z, w_ft, b_f_bias[j], qw, b_kn_w[j])
            o = _fox_attn(qt, kt, vt, z, cum)
            w_out = b_w_out[j].astype(BF16)
        x = _out_proj(o, x, mods[i], w_out, final_norm_w, final)
    return x
```

```python
import functools

import jax
import jax.numpy as jnp
from jax import lax
from jax.experimental import pallas as pl
from jax.experimental.pallas import tpu as pltpu

F32 = jnp.float32
BF16 = jnp.bfloat16

D_MODEL = 1024
DEPTH = 4
EPS = 1e-6
CHUNK = 64

GDN_QK_HEADS = 8
GDN_V_HEADS = 16
GDN_HEAD_DIM = 128
GDN_QK_WIDTH = GDN_QK_HEADS * GDN_HEAD_DIM
GDN_V_WIDTH = GDN_V_HEADS * GDN_HEAD_DIM
GDN_CONV_CH = 2 * GDN_QK_WIDTH + GDN_V_WIDTH
CONV_WIDTH = 4

FOX_HEADS = 16
FOX_HEAD_DIM = 64
FOX_WIDTH = FOX_HEADS * FOX_HEAD_DIM

LANES = 128
SUBLANES = 8
BF16_ROWS = 16
LOG2E = 1.4426950408889634
VMEM_LIMIT = 48 * 1024 * 1024

TM_PROJ = 512
TM_GDN_IN = 256
TC_DELTA = 256
DELTA_CHUNKS_PER_TRIP = 2
TQ_ATTN = 256
CONV_GROUP = 512
FOX_GROUP = 512
KT_BLOCK = 512
ATTN_UNROLL = 40
ATTN_UNROLL_DIAG = 16
ST_RING = 4
P_RING = 2


def _sigmoid(x):
    return 1.0 / (1.0 + jnp.exp(-x))


def _silu(x):
    return x * _sigmoid(x)


def _softplus(x):
    return jnp.maximum(x, 0.0) + jnp.log(1.0 + jnp.exp(-jnp.abs(x)))


def _split3(a):
    hi = a.astype(BF16)
    r = a - hi.astype(F32)
    mid = r.astype(BF16)
    lo = (r - mid.astype(F32)).astype(BF16)
    return hi, mid, lo


def _dot(a, b):
    return jnp.dot(a, b, preferred_element_type=F32)


def _dot_nt(a, b):
    return lax.dot_general(a, b, (((1,), (1,)), ((), ())), preferred_element_type=F32)


def _dot_tn(a, b):
    return lax.dot_general(a, b, (((0,), (0,)), ((), ())), preferred_element_type=F32)


def _modulated_norm(x, mod, nw):
    ms = jnp.mean(x * x, axis=-1, keepdims=True)
    y = x * lax.rsqrt(ms + EPS) * nw
    return y * (1.0 + mod[1:2, :]) + mod[0:1, :]


def _params(*sem):
    return pltpu.CompilerParams(dimension_semantics=sem, vmem_limit_bytes=VMEM_LIMIT)


def _resident(shape):
    nd = len(shape)
    return pl.BlockSpec(shape, lambda *_: (0,) * nd, pipeline_mode=pl.Buffered(1))


def _ada_kernel(c_ref, w_ref, b_ref, o_ref):
    cond = _silu(c_ref[...])
    o_ref[0] = _dot(cond, w_ref[0]) + b_ref[0]


def _ada_mod(c, ada_w, ada_b):
    B, D = c.shape
    depth = ada_w.shape[0]
    out = pl.pallas_call(
        _ada_kernel,
        grid=(depth,),
        in_specs=[
            pl.BlockSpec((B, D), lambda i: (0, 0)),
            pl.BlockSpec((1, D, 3 * D), lambda i: (i, 0, 0)),
            pl.BlockSpec((1, 1, 3 * D), lambda i: (i, 0, 0)),
        ],
        out_specs=pl.BlockSpec((1, B, 3 * D), lambda i: (i, 0, 0)),
        out_shape=jax.ShapeDtypeStruct((depth, B, 3 * D), F32),
        compiler_params=_params("arbitrary"),
        name="ada_mod",
    )(c, ada_w, ada_b.reshape(depth, 1, 3 * D))
    return out.reshape(depth, B, 3, D)


def _gdn_in_kernel(x_ref, mod_ref, nw_ref, w_ref, wba_ref, wbat_ref, cw_ref,
                   alog_ref, dtb_ref, alogc_ref, dtbc_ref,
                   q_ref, k_ref, kt_ref, v_ref, z_ref, bg_ref, gt_ref, cbuf_ref, *, tm):
    s = pl.program_id(1)
    tail = SUBLANES
    nchunk = tm // CHUNK

    @pl.when(s == 0)
    def _():
        cbuf_ref[0:tail, :] = jnp.zeros((tail, GDN_CONV_CH), F32)

    h = _modulated_norm(x_ref[0], mod_ref[0, 0], nw_ref[...])
    hb = h.astype(BF16)

    qscale = GDN_HEAD_DIM ** -0.5
    for c in range(GDN_CONV_CH // CONV_GROUP):
        lo = c * CONV_GROUP
        pre = _dot(hb, w_ref[:, lo:lo + CONV_GROUP])
        cbuf_ref[tail:tail + tm, lo:lo + CONV_GROUP] = pre
        acc = cw_ref[CONV_WIDTH - 1:CONV_WIDTH, lo:lo + CONV_GROUP] * pre
        for j in range(CONV_WIDTH - 1):
            off = tail - (CONV_WIDTH - 1) + j
            acc = acc + cw_ref[j:j + 1, lo:lo + CONV_GROUP] * cbuf_ref[off:off + tm, lo:lo + CONV_GROUP]
        y = _silu(acc)
        if lo < 2 * GDN_QK_WIDTH:
            for hh in range(CONV_GROUP // GDN_HEAD_DIM):
                seg = y[:, hh * GDN_HEAD_DIM:(hh + 1) * GDN_HEAD_DIM]
                inv = lax.rsqrt(jnp.sum(seg * seg, axis=-1, keepdims=True) + EPS)
                col = lo + hh * GDN_HEAD_DIM
                if col < GDN_QK_WIDTH:
                    q_ref[0, :, col:col + GDN_HEAD_DIM] = (seg * inv * qscale).astype(BF16)
                else:
                    col -= GDN_QK_WIDTH
                    kn = seg * inv
                    k_ref[0, :, col:col + GDN_HEAD_DIM] = kn.astype(BF16)
                    knt = kn.T.astype(BF16)
                    for cc in range(nchunk):
                        kt_ref[0, cc, col:col + GDN_HEAD_DIM, :] = knt[:, cc * CHUNK:(cc + 1) * CHUNK]
        else:
            col = lo - 2 * GDN_QK_WIDTH
            v_ref[0, :, col:col + CONV_GROUP] = y.astype(BF16)
    cbuf_ref[0:tail, :] = cbuf_ref[tm:tm + tail, :]

    for c in range(GDN_V_WIDTH // CONV_GROUP):
        lo = c * CONV_GROUP
        z = _dot(hb, w_ref[:, GDN_CONV_CH + lo:GDN_CONV_CH + lo + CONV_GROUP])
        z_ref[0, :, lo:lo + CONV_GROUP] = z.astype(BF16)

    nh = GDN_V_HEADS
    ba = _dot(hb, wba_ref[...])
    beta = _sigmoid(ba[:, 0:nh])
    g = -jnp.exp(alog_ref[...]) * _softplus(ba[:, LANES:LANES + nh] + dtb_ref[...])
    row = lax.broadcasted_iota(jnp.int32, (tm, tm), 0)
    col = lax.broadcasted_iota(jnp.int32, (tm, tm), 1)
    same = (row // CHUNK) == (col // CHUNK)
    tri_lo = jnp.where(same & (col <= row), 1.0, 0.0).astype(BF16)
    tri_up = jnp.where(same & (row <= col), 1.0, 0.0).astype(BF16)
    gc = sum(_dot(tri_lo, p) for p in _split3(g))
    bg_ref[0, :, 0:nh] = beta
    bg_ref[0, :, nh:2 * nh] = gc
    bat = _dot_nt(wbat_ref[...], hb)
    g_t = -jnp.exp(alogc_ref[...]) * _softplus(bat + dtbc_ref[...])
    gct = sum(_dot(p, tri_up) for p in _split3(g_t))
    for cc in range(nchunk):
        gt_ref[0, cc] = gct[:, cc * CHUNK:(cc + 1) * CHUNK]


def _gdn_in(x, mod, nw, w_main, w_ba, w_at, conv_w, a_log, dt_bias):
    B, S, D = x.shape
    tm = TM_GDN_IN
    nh = GDN_V_HEADS
    kern = functools.partial(_gdn_in_kernel, tm=tm)
    tok = lambda width: pl.BlockSpec((1, tm, width), lambda b, s: (b, s, 0))
    return pl.pallas_call(
        kern,
        grid=(B, S // tm),
        in_specs=[
            tok(D),
            pl.BlockSpec((1, 1, 3, D), lambda b, s: (b, 0, 0, 0)),
            _resident((1, D)),
            _resident(w_main.shape),
            _resident(w_ba.shape),
            _resident(w_at.shape),
            _resident(conv_w.shape),
            _resident((1, nh)), _resident((1, nh)), _resident((nh, 1)), _resident((nh, 1)),
        ],
        out_specs=[
            tok(GDN_QK_WIDTH), tok(GDN_QK_WIDTH),
            pl.BlockSpec((1, tm // CHUNK, GDN_QK_WIDTH, CHUNK), lambda b, s: (b, s, 0, 0)),
            tok(GDN_V_WIDTH), tok(GDN_V_WIDTH),
            tok(2 * nh),
            pl.BlockSpec((1, tm // CHUNK, nh, CHUNK), lambda b, s: (b, s, 0, 0)),
        ],
        out_shape=[
            jax.ShapeDtypeStruct((B, S, GDN_QK_WIDTH), BF16),
            jax.ShapeDtypeStruct((B, S, GDN_QK_WIDTH), BF16),
            jax.ShapeDtypeStruct((B, S // CHUNK, GDN_QK_WIDTH, CHUNK), BF16),
            jax.ShapeDtypeStruct((B, S, GDN_V_WIDTH), BF16),
            jax.ShapeDtypeStruct((B, S, GDN_V_WIDTH), BF16),
            jax.ShapeDtypeStruct((B, S, 2 * nh), F32),
            jax.ShapeDtypeStruct((B, S // CHUNK, nh, CHUNK), F32),
        ],
        scratch_shapes=[pltpu.VMEM((tm + SUBLANES, GDN_CONV_CH), F32)],
        compiler_params=_params("arbitrary", "arbitrary"),
        name="gdn_in",
    )(x, mod.reshape(B, 1, 3, D), nw.reshape(1, D), w_main, w_ba, w_at, conv_w,
      a_log.reshape(1, nh), dt_bias.reshape(1, nh), a_log.reshape(nh, 1), dt_bias.reshape(nh, 1))


def _delta_kernel(q_ref, k_ref, kt_ref, v_ref, z_ref, bg_ref, gt_ref, nw_ref, o_ref, s_ref, *, tc):
    s = pl.program_id(1)
    C = CHUNK
    dh = GDN_HEAD_DIM
    nqk = GDN_QK_HEADS
    nv = GDN_V_HEADS
    rep = nv // nqk
    heads = range(nv)

    @pl.when(s == 0)
    def _():
        s_ref[...] = jnp.zeros(s_ref.shape, F32)

    row = lax.broadcasted_iota(jnp.int32, (C, C), 0)
    col = lax.broadcasted_iota(jnp.int32, (C, C), 1)
    lower = row >= col
    strict = row > col
    eye = jnp.where(row == col, 1.0, 0.0).astype(F32)
    nw = nw_ref[...]

    nper = DELTA_CHUNKS_PER_TRIP
    units = [(t, h) for t in range(nper) for h in heads]

    def trip(i, _):
        cs = [i * nper + t for t in range(nper)]
        rows = [pl.ds(pl.multiple_of(c * C, C), C) for c in cs]
        bg = [bg_ref[0, rows[t], :] for t in range(nper)]
        gt = [gt_ref[0, cs[t]] for t in range(nper)]
        groups = [(t, j) for t in range(nper) for j in range(nqk)]
        qb = {(t, j): q_ref[0, rows[t], j * dh:(j + 1) * dh] for t, j in groups}
        kb = {(t, j): k_ref[0, rows[t], j * dh:(j + 1) * dh] for t, j in groups}
        ktb = {(t, j): kt_ref[0, cs[t], j * dh:(j + 1) * dh, :] for t, j in groups}
        kq = {g: _dot(jnp.concatenate([kb[g], qb[g]], axis=0), ktb[g]) for g in groups}
        grp = {(t, h): (t, h // rep) for t, h in units}

        beta = {(t, h): bg[t][:, h:h + 1] for t, h in units}
        gc = {(t, h): bg[t][:, nv + h:nv + h + 1] for t, h in units}
        gr = {(t, h): gt[t][h:h + 1, :] for t, h in units}
        g_last = {u: gr[u][:, C - 1:C] for u in units}
        decay = {u: jnp.exp(jnp.where(lower, gc[u] - gr[u], -jnp.inf)) for u in units}
        egc = {u: jnp.exp(gc[u]) for u in units}

        X = {u: -(jnp.where(strict, kq[grp[u]][:C] * decay[u], 0.0) * beta[u]) for u in units}
        negL = X
        P = {u: eye + X[u] for u in units}
        Xb = {u: X[u].astype(BF16) for u in units}
        X = {u: _dot(Xb[u], Xb[u]) for u in units}
        p = 4
        while p < C // 2:
            Xb = {u: X[u].astype(BF16) for u in units}
            PX = {u: _dot(jnp.concatenate([P[u], X[u]], axis=0).astype(BF16), Xb[u]) for u in units}
            P = {u: P[u] + PX[u][:C] for u in units}
            X = {u: PX[u][C:] for u in units}
            p *= 2
        P = {u: P[u] + _dot(P[u].astype(BF16), X[u].astype(BF16)) for u in units}
        Tb, rest = {}, {}
        for u in units:
            Tb[u] = P[u].astype(BF16)
            nl_hi = negL[u].astype(BF16)
            nl_lo = (negL[u] - nl_hi.astype(F32)).astype(BF16)
            hl = _dot(jnp.concatenate([nl_hi, nl_lo], axis=0), Tb[u])
            rest[u] = (eye - Tb[u].astype(F32)) + (hl[:C] + hl[C:])
        P = {u: Tb[u].astype(F32) + _dot(Tb[u], rest[u].astype(BF16)) for u in units}

        uw = {}
        for t, h in units:
            u = (t, h)
            vf = v_ref[0, rows[t], h * dh:(h + 1) * dh].astype(F32)
            kf = kb[grp[u]].astype(F32)
            rhs = jnp.concatenate([vf * beta[u], kf * (beta[u] * egc[u])], axis=1).astype(BF16)
            uw[u] = _dot(P[u].astype(BF16), rhs)
        wq = {u: jnp.concatenate([uw[u][:, dh:], qb[grp[u]].astype(F32) * egc[u]], axis=0).astype(BF16)
              for u in units}
        ak_lhs = {}
        for u in units:
            attn = (kq[grp[u]][C:] * decay[u]).astype(BF16)
            kdt = (ktb[grp[u]].astype(F32) * jnp.exp(g_last[u] - gr[u])).astype(BF16)
            ak_lhs[u] = jnp.concatenate([attn, kdt], axis=0)
        for t in range(nper):
            S = [s_ref[h] for h in heads]
            r = [_dot(wq[(t, h)], S[h].astype(BF16)) for h in heads]
            vnb = [(uw[(t, h)][:, :dh] - r[h][:C]).astype(BF16) for h in heads]
            ak = [_dot(ak_lhs[(t, h)], vnb[h]) for h in heads]
            for h in heads:
                s_ref[h] = S[h] * jnp.exp(g_last[(t, h)]) + ak[h][C:]
            for h in heads:
                o = r[h][C:] + ak[h][:C]
                var = jnp.mean(o * o, axis=-1, keepdims=True)
                zf = z_ref[0, rows[t], h * dh:(h + 1) * dh].astype(F32)
                o_ref[0, rows[t], h * dh:(h + 1) * dh] = (o * lax.rsqrt(var + EPS) * nw * _silu(zf)).astype(BF16)
        return 0

    lax.fori_loop(0, tc // (C * nper), trip, 0)


def _gdn_delta(q, k, kt, v, z, bg, gt, nw):
    B, S, _ = q.shape
    tc = TC_DELTA
    nv = GDN_V_HEADS
    dh = GDN_HEAD_DIM
    kern = functools.partial(_delta_kernel, tc=tc)
    tok = lambda width: pl.BlockSpec((1, tc, width), lambda b, s: (b, s, 0))
    return pl.pallas_call(
        kern,
        grid=(B, S // tc),
        in_specs=[
            tok(GDN_QK_WIDTH), tok(GDN_QK_WIDTH),
            pl.BlockSpec((1, tc // CHUNK, GDN_QK_WIDTH, CHUNK), lambda b, s: (b, s, 0, 0)),
            tok(GDN_V_WIDTH), tok(GDN_V_WIDTH),
            tok(2 * nv),
            pl.BlockSpec((1, tc // CHUNK, nv, CHUNK), lambda b, s: (b, s, 0, 0)),
            pl.BlockSpec((1, dh), lambda b, s: (0, 0)),
        ],
        out_specs=tok(GDN_V_WIDTH),
        out_shape=jax.ShapeDtypeStruct((B, S, GDN_V_WIDTH), BF16),
        scratch_shapes=[pltpu.VMEM((nv, dh, dh), F32)],
        compiler_params=_params("arbitrary", "arbitrary"),
        name="gdn_delta",
    )(q, k, kt, v, z, bg, gt, nw.reshape(1, dh))


def _fox_in_kernel(x_ref, mod_ref, nw_ref, wt_ref, wz_ref, wft_ref, fb_ref, qw_ref, kw_ref,
                   qt_ref, kt_ref, vt_ref, z_ref, cum_ref, carry_ref, *, tm):
    s = pl.program_id(1)
    W = FOX_WIDTH
    dh = FOX_HEAD_DIM

    @pl.when(s == 0)
    def _():
        carry_ref[...] = jnp.zeros(carry_ref.shape, F32)

    h = _modulated_norm(x_ref[0], mod_ref[0, 0], nw_ref[...])
    hb = h.astype(BF16)

    grp = FOX_GROUP
    for g in range(3 * W // grp):
        blk = _dot_nt(wt_ref[g * grp:(g + 1) * grp, :], hb)
        r0 = (g * grp) % W
        if g * grp < 2 * W:
            is_q = g * grp < W
            wcol = qw_ref[...] if is_q else kw_ref[...]
            dst = qt_ref if is_q else kt_ref
            for hh in range(grp // dh):
                seg = blk[hh * dh:(hh + 1) * dh]
                inv = lax.rsqrt(jnp.mean(seg * seg, axis=0, keepdims=True) + EPS)
                dst[0, r0 + hh * dh:r0 + (hh + 1) * dh, :] = (seg * inv * wcol).astype(BF16)
        else:
            vt_ref[0, r0:r0 + grp, :] = blk.astype(BF16)
    z_ref[0] = _dot(hb, wz_ref[...]).astype(BF16)

    xf = _dot_nt(wft_ref[...], hb) + fb_ref[...]
    log_f = -_softplus(-xf)
    row = lax.broadcasted_iota(jnp.int32, (tm, tm), 0)
    col = lax.broadcasted_iota(jnp.int32, (tm, tm), 1)
    tri_up = jnp.where(row <= col, 1.0, 0.0).astype(BF16)
    cum = sum(_dot(p, tri_up) for p in _split3(log_f)) + carry_ref[...]
    cum_ref[0] = cum
    carry_ref[...] = cum[:, tm - 1:tm]


def _fox_in(x, mod, nw, w_qkv_t, w_z, w_ft, f_bias, qw, kw):
    B, S, D = x.shape
    tm = TM_PROJ
    H = FOX_HEADS
    dh = FOX_HEAD_DIM
    W = FOX_WIDTH
    kern = functools.partial(_fox_in_kernel, tm=tm)
    feat_major = pl.BlockSpec((1, W, tm), lambda b, s: (b, 0, s))
    fm_shape = jax.ShapeDtypeStruct((B, W, S), BF16)
    return pl.pallas_call(
        kern,
        grid=(B, S // tm),
        in_specs=[
            pl.BlockSpec((1, tm, D), lambda b, s: (b, s, 0)),
            pl.BlockSpec((1, 1, 3, D), lambda b, s: (b, 0, 0, 0)),
            _resident((1, D)),
            _resident(w_qkv_t.shape),
            _resident(w_z.shape),
            _resident(w_ft.shape),
            _resident((H, 1)),
            _resident((dh, 1)),
            _resident((dh, 1)),
        ],
        out_specs=[
            feat_major, feat_major, feat_major,
            pl.BlockSpec((1, tm, W), lambda b, s: (b, s, 0)),
            pl.BlockSpec((1, H, tm), lambda b, s: (b, 0, s)),
        ],
        out_shape=[
            fm_shape, fm_shape, fm_shape,
            jax.ShapeDtypeStruct((B, S, W), BF16),
            jax.ShapeDtypeStruct((B, H, S), F32),
        ],
        scratch_shapes=[pltpu.VMEM((H, 1), F32)],
        compiler_params=_params("arbitrary", "arbitrary"),
        name="fox_in",
    )(x, mod.reshape(B, 1, 3, D), nw.reshape(1, D), w_qkv_t, w_z, w_ft, f_bias.reshape(H, 1),
      qw.reshape(dh, 1), kw.reshape(dh, 1))


def _fox_attn_kernel(qt_ref, kt_ref, vt_ref, z_ref, cum_ref, o_ref,
                     qa_ref, ka_ref, va_ref, r_ref, m_ref, acc_ref, st_ref, p_ref, *, tq, seq):
    hp = pl.program_id(1)
    dh = FOX_HEAD_DIM
    pair = LANES // dh
    nq = seq // tq
    nsplit = 3
    heads = range(pair)
    kidx = lax.broadcasted_iota(jnp.int32, (tq, tq), 0)
    qidx = lax.broadcasted_iota(jnp.int32, (tq, tq), 1)
    causal = kidx <= qidx

    aug_row = lax.broadcasted_iota(jnp.int32, (BF16_ROWS, seq), 0)
    pick = jnp.where(aug_row < nsplit, 1.0, 0.0).astype(BF16)
    one_row = jnp.where(aug_row < 1, 1.0, 0.0).astype(BF16)
    for e in heads:
        qa_ref[e, 0:dh, :] = qt_ref[0, e * dh:(e + 1) * dh, :]
        qa_ref[e, dh:dh + BF16_ROWS, :] = pick
        qa_ref[e, dh + BF16_ROWS:, :] = jnp.zeros((LANES - dh - BF16_ROWS, seq), BF16)
        va_ref[e, 0:dh, :] = vt_ref[0, e * dh:(e + 1) * dh, :]
        va_ref[e, dh:, :] = one_row
        crow = cum_ref[0, pl.ds(pair * hp + e, 1), :]
        firsts = [jnp.broadcast_to(crow[:, j * tq:j * tq + 1], (1, tq)) for j in range(nq)]
        for j in range(nq):
            r_ref[e, j] = firsts[j] * (-LOG2E)
        rel = (crow - jnp.concatenate(firsts, axis=1)) * (-LOG2E)
        parts = [p.astype(F32) for p in _split3(rel)]
        btile = jnp.concatenate(parts + [jnp.zeros((SUBLANES - nsplit, seq), F32)], axis=0)
        for blk in range(seq // KT_BLOCK):
            sl = slice(blk * KT_BLOCK, (blk + 1) * KT_BLOCK)
            top = jnp.concatenate([kt_ref[0, e * dh:(e + 1) * dh, sl].astype(F32), btile[:, sl],
                                   jnp.zeros((LANES - dh - SUBLANES, KT_BLOCK), F32)], axis=0)
            ka_ref[e, sl, :] = top.T.astype(BF16)

    m_ref[...] = jnp.full(m_ref.shape, -jnp.inf, F32)
    acc_ref[...] = jnp.zeros(acc_ref.shape, F32)
    p_ref[...] = jnp.zeros(p_ref.shape, BF16)

    def scores(e, qi, kj):
        return _dot(ka_ref[e, pl.ds(pl.multiple_of(kj * tq, tq), tq), :],
                    qa_ref[e, :, pl.ds(pl.multiple_of(qi * tq, tq), tq)])

    def weighted_values(e, kj, p):
        return _dot(va_ref[e, :, pl.ds(pl.multiple_of(kj * tq, tq), tq)], p)

    def softmax_update(e, qi, kj, read_scores):
        off = r_ref[e, kj]
        m_rel = m_ref[e, qi] - off
        m_new = jnp.maximum(m_rel, jnp.max(read_scores(), axis=0, keepdims=True))
        m_ref[e, qi] = m_new + off
        return jnp.exp2(m_rel - m_new), jnp.exp2(read_scores() - m_new).astype(BF16)

    def mask_diagonal(st, masked):
        return jnp.where(causal, st, -jnp.inf) if masked else st

    def simple_step(pr, masked):
        qi, kj = pr
        for e in heads:
            st = mask_diagonal(scores(e, qi, kj), masked)
            alpha, p = softmax_update(e, qi, kj, lambda st=st: st)
            acc_ref[e, qi] = alpha * acc_ref[e, qi] + weighted_values(e, kj, p)

    def sweep(first, advance, npairs, masked, unroll):
        n_main = (npairs // unroll) * unroll
        cur = first
        if n_main:
            second = advance(*first)
            for e in heads:
                st_ref[e, 0] = scores(e, *first)
                st_ref[e, 1] = scores(e, *second)

            def body(_, carry):
                prev, cur, nxt, a_prev = carry
                for i in range(unroll):
                    nxt2 = advance(*nxt)
                    new_a = []
                    for e in heads:
                        pv_prev = weighted_values(e, prev[1], p_ref[e, (i - 1) % P_RING])
                        st_ref[e, (i + 2) % ST_RING] = scores(e, *nxt2)
                        alpha, p = softmax_update(
                            e, cur[0], cur[1], lambda e=e, i=i: mask_diagonal(st_ref[e, i % ST_RING], masked))
                        p_ref[e, i % P_RING] = p
                        acc_ref[e, prev[0]] = a_prev[e] * acc_ref[e, prev[0]] + pv_prev
                        new_a.append(alpha)
                    prev, cur, nxt, a_prev = cur, nxt, nxt2, tuple(new_a)
                return prev, cur, nxt, a_prev

            ones = tuple(jnp.ones((1, tq), F32) for _ in heads)
            spare = (jnp.int32(nq), jnp.int32(0))
            prev, cur, _, a_prev = lax.fori_loop(0, n_main // unroll, body, (spare, first, second, ones))
            for e in heads:
                pv_prev = weighted_values(e, prev[1], p_ref[e, (unroll - 1) % P_RING])
                acc_ref[e, prev[0]] = a_prev[e] * acc_ref[e, prev[0]] + pv_prev

        def tail(_, pr):
            simple_step(pr, masked)
            return advance(*pr)

        lax.fori_loop(0, npairs - n_main, tail, cur)

    def next_off_diagonal(qi, kj):
        k2 = kj + 1
        wrap = k2 >= qi
        return jnp.where(wrap, jnp.minimum(qi + 1, nq - 1), qi), jnp.where(wrap, 0, k2)

    def next_diagonal(qi, kj):
        nxt = jnp.minimum(qi + 1, nq - 1)
        return nxt, nxt

    zero = jnp.int32(0)
    sweep((jnp.int32(min(1, nq - 1)), zero), next_off_diagonal, nq * (nq - 1) // 2, False, ATTN_UNROLL)
    sweep((zero, zero), next_diagonal, nq, True, ATTN_UNROLL_DIAG)

    def finish(qi, _):
        accs = [acc_ref[e, qi] for e in heads]
        o = jnp.concatenate([a[0:dh] * (1.0 / a[dh:dh + 1]) for a in accs], axis=0).T
        rows = pl.ds(pl.multiple_of(qi * tq, tq), tq)
        zf = z_ref[0, rows, :].astype(F32)
        o_ref[0, rows, :] = (o * _silu(zf)).astype(BF16)
        return 0

    lax.fori_loop(0, nq, finish, 0)


def _fox_attn(qt, kt, vt, z, cum):
    B, W, S = qt.shape
    H = FOX_HEADS
    dh = FOX_HEAD_DIM
    pair = LANES // dh
    tq = TQ_ATTN
    nq = S // tq
    kern = functools.partial(_fox_attn_kernel, tq=tq, seq=S)
    fm = pl.BlockSpec((1, LANES, S), lambda b, h: (b, h, 0))
    return pl.pallas_call(
        kern,
        grid=(B, H // pair),
        in_specs=[
            fm, fm, fm,
            pl.BlockSpec((1, S, LANES), lambda b, h: (b, 0, h)),
            pl.BlockSpec((1, H, S), lambda b, h: (b, 0, 0)),
        ],
        out_specs=pl.BlockSpec((1, S, LANES), lambda b, h: (b, 0, h)),
        out_shape=jax.ShapeDtypeStruct((B, S, W), BF16),
        scratch_shapes=[
            pltpu.VMEM((pair, LANES, S), BF16),
            pltpu.VMEM((pair, S, LANES), BF16),
            pltpu.VMEM((pair, dh + BF16_ROWS, S), BF16),
            pltpu.VMEM((pair, nq, 1, tq), F32),
            pltpu.VMEM((pair, nq, 1, tq), F32),
            pltpu.VMEM((pair, nq + 1, dh + BF16_ROWS, tq), F32),
            pltpu.VMEM((pair, ST_RING, tq, tq), F32),
            pltpu.VMEM((pair, P_RING, tq, tq), BF16),
        ],
        compiler_params=_params("arbitrary", "arbitrary"),
        name="fox_attn",
    )(qt, kt, vt, z, cum)


def _out_kernel(o_ref, x_ref, mod_ref, w_ref, fnw_ref, y_ref, *, final):
    y = _dot(o_ref[0], w_ref[...])
    xn = x_ref[0] + mod_ref[0, 0][2:3, :] * y
    if final:
        ms = jnp.mean(xn * xn, axis=-1, keepdims=True)
        xn = xn * lax.rsqrt(ms + EPS) * fnw_ref[...]
    y_ref[0] = xn


def _out_proj(o, x, mod, w_out, fnw, final):
    B, S, D = x.shape
    width = o.shape[-1]
    tm = TM_PROJ
    kern = functools.partial(_out_kernel, final=final)
    return pl.pallas_call(
        kern,
        grid=(B, S // tm),
        in_specs=[
            pl.BlockSpec((1, tm, width), lambda b, s: (b, s, 0)),
            pl.BlockSpec((1, tm, D), lambda b, s: (b, s, 0)),
            pl.BlockSpec((1, 1, 3, D), lambda b, s: (b, 0, 0, 0)),
            _resident(w_out.shape),
            _resident((1, D)),
        ],
        out_specs=pl.BlockSpec((1, tm, D), lambda b, s: (b, s, 0)),
        out_shape=jax.ShapeDtypeStruct((B, S, D), F32),
        compiler_params=_params("arbitrary", "arbitrary"),
        name="out_proj",
    )(o, x, mod.reshape(B, 1, 3, D), w_out, fnw.reshape(1, D))


def kernel(x, c, norm_w, ada_w, ada_b, a_w_in, a_conv_w, a_A_log, a_dt_bias, a_norm_w, a_w_out,
           b_w_in, b_f_bias, b_qn_w, b_kn_w, b_w_out, final_norm_w):
    B, S, D = x.shape
    assert D == D_MODEL and S % max(TM_PROJ, TM_GDN_IN, TC_DELTA, TQ_ATTN) == 0
    mods = _ada_mod(c, ada_w, ada_b)
    nh = GDN_V_HEADS
    for i in range(DEPTH):
        j = i // 2
        final = i == DEPTH - 1
        if i % 2 == 0:
            w_in = a_w_in[j]
            w_main = w_in[:, :GDN_CONV_CH + GDN_V_WIDTH].astype(BF16)
            w_b = w_in[:, GDN_CONV_CH + GDN_V_WIDTH:GDN_CONV_CH + GDN_V_WIDTH + nh]
            w_a = w_in[:, GDN_CONV_CH + GDN_V_WIDTH + nh:]
            pad = jnp.zeros((D, LANES - nh), F32)
            w_ba = jnp.concatenate([w_b, pad, w_a, pad], axis=1).astype(BF16)
            w_at = w_a.T.astype(BF16)
            q, k, kt, v, z, bg, gt = _gdn_in(x, mods[i], norm_w[i], w_main, w_ba, w_at, a_conv_w[j],
                                              a_A_log[j], a_dt_bias[j])
            o = _gdn_delta(q, k, kt, v, z, bg, gt, a_norm_w[j])
            w_out = a_w_out[j].astype(BF16)
        else:
            w_in = b_w_in[j]
            w_qkv_t = w_in[:, :3 * FOX_WIDTH].T.astype(BF16)
            w_z = w_in[:, 3 * FOX_WIDTH:4 * FOX_WIDTH].astype(BF16)
            w_ft = w_in[:, 4 * FOX_WIDTH:].T.astype(BF16)
            qw = b_qn_w[j] * (FOX_HEAD_DIM ** -0.5 * LOG2E)
            qt, kt, vt, z, cum = _fox_in(x, mods[i], norm_w[i], w_qkv_t, w_z, w_ft, b_f_bias[j], qw, b_kn_w[j])
            o = _fox_attn(qt, kt, vt, z, cum)
            w_out = b_w_out[j].astype(BF16)
        x = _out_proj(o, x, mods[i], w_out, final_norm_w, final)
    return x
```

```python
import functools

import jax
import jax.numpy as jnp
from jax import lax
from jax.experimental import pallas as pl
from jax.experimental.pallas import tpu as pltpu

F32 = jnp.float32
BF16 = jnp.bfloat16

D_MODEL = 1024
DEPTH = 4
EPS = 1e-6
CHUNK = 64

GDN_QK_HEADS = 8
GDN_V_HEADS = 16
GDN_HEAD_DIM = 128
GDN_QK_WIDTH = GDN_QK_HEADS * GDN_HEAD_DIM
GDN_V_WIDTH = GDN_V_HEADS * GDN_HEAD_DIM
GDN_CONV_CH = 2 * GDN_QK_WIDTH + GDN_V_WIDTH
CONV_WIDTH = 4

FOX_HEADS = 16
FOX_HEAD_DIM = 64
FOX_WIDTH = FOX_HEADS * FOX_HEAD_DIM

LANES = 128
SUBLANES = 8
BF16_ROWS = 16
LOG2E = 1.4426950408889634
VMEM_LIMIT = 48 * 1024 * 1024

TM_PROJ = 512
TM_GDN_IN = 256
TC_DELTA = 256
DELTA_CHUNKS_PER_TRIP = 4
TQ_ATTN = 256
CONV_GROUP = 512
FOX_GROUP = 512
KT_BLOCK = 512
ATTN_UNROLL = 40
ATTN_UNROLL_DIAG = 16
ST_RING = 4
P_RING = 2


def _sigmoid(x):
    return 1.0 / (1.0 + jnp.exp(-x))


def _silu(x):
    return x * _sigmoid(x)


def _softplus(x):
    return jnp.maximum(x, 0.0) + jnp.log(1.0 + jnp.exp(-jnp.abs(x)))


def _split3(a):
    hi = a.astype(BF16)
    r = a - hi.astype(F32)
    mid = r.astype(BF16)
    lo = (r - mid.astype(F32)).astype(BF16)
    return hi, mid, lo


def _dot(a, b):
    return jnp.dot(a, b, preferred_element_type=F32)


def _dot_nt(a, b):
    return lax.dot_general(a, b, (((1,), (1,)), ((), ())), preferred_element_type=F32)


def _dot_tn(a, b):
    return lax.dot_general(a, b, (((0,), (0,)), ((), ())), preferred_element_type=F32)


def _modulated_norm(x, mod, nw):
    ms = jnp.mean(x * x, axis=-1, keepdims=True)
    y = x * lax.rsqrt(ms + EPS) * nw
    return y * (1.0 + mod[1:2, :]) + mod[0:1, :]


def _params(*sem):
    return pltpu.CompilerParams(dimension_semantics=sem, vmem_limit_bytes=VMEM_LIMIT)


def _resident(shape):
    nd = len(shape)
    return pl.BlockSpec(shape, lambda *_: (0,) * nd, pipeline_mode=pl.Buffered(1))


def _ada_kernel(c_ref, w_ref, b_ref, o_ref):
    cond = _silu(c_ref[...])
    o_ref[0] = _dot(cond, w_ref[0]) + b_ref[0]


def _ada_mod(c, ada_w, ada_b):
    B, D = c.shape
    depth = ada_w.shape[0]
    out = pl.pallas_call(
        _ada_kernel,
        grid=(depth,),
        in_specs=[
            pl.BlockSpec((B, D), lambda i: (0, 0)),
            pl.BlockSpec((1, D, 3 * D), lambda i: (i, 0, 0)),
            pl.BlockSpec((1, 1, 3 * D), lambda i: (i, 0, 0)),
        ],
        out_specs=pl.BlockSpec((1, B, 3 * D), lambda i: (i, 0, 0)),
        out_shape=jax.ShapeDtypeStruct((depth, B, 3 * D), F32),
        compiler_params=_params("arbitrary"),
        name="ada_mod",
    )(c, ada_w, ada_b.reshape(depth, 1, 3 * D))
    return out.reshape(depth, B, 3, D)


def _gdn_in_kernel(x_ref, mod_ref, nw_ref, w_ref, wba_ref, wbat_ref, cw_ref,
                   alog_ref, dtb_ref, alogc_ref, dtbc_ref,
                   q_ref, k_ref, kt_ref, v_ref, z_ref, bg_ref, gt_ref, cbuf_ref, *, tm):
    s = pl.program_id(1)
    tail = SUBLANES
    nchunk = tm // CHUNK

    @pl.when(s == 0)
    def _():
        cbuf_ref[0:tail, :] = jnp.zeros((tail, GDN_CONV_CH), F32)

    h = _modulated_norm(x_ref[0], mod_ref[0, 0], nw_ref[...])
    hb = h.astype(BF16)

    qscale = GDN_HEAD_DIM ** -0.5
    for c in range(GDN_CONV_CH // CONV_GROUP):
        lo = c * CONV_GROUP
        pre = _dot(hb, w_ref[:, lo:lo + CONV_GROUP])
        cbuf_ref[tail:tail + tm, lo:lo + CONV_GROUP] = pre
        acc = cw_ref[CONV_WIDTH - 1:CONV_WIDTH, lo:lo + CONV_GROUP] * pre
        for j in range(CONV_WIDTH - 1):
            off = tail - (CONV_WIDTH - 1) + j
            acc = acc + cw_ref[j:j + 1, lo:lo + CONV_GROUP] * cbuf_ref[off:off + tm, lo:lo + CONV_GROUP]
        y = _silu(acc)
        if lo < 2 * GDN_QK_WIDTH:
            for hh in range(CONV_GROUP // GDN_HEAD_DIM):
                seg = y[:, hh * GDN_HEAD_DIM:(hh + 1) * GDN_HEAD_DIM]
                inv = lax.rsqrt(jnp.sum(seg * seg, axis=-1, keepdims=True) + EPS)
                col = lo + hh * GDN_HEAD_DIM
                if col < GDN_QK_WIDTH:
                    q_ref[0, :, col:col + GDN_HEAD_DIM] = (seg * inv * qscale).astype(BF16)
                else:
                    col -= GDN_QK_WIDTH
                    kn = seg * inv
                    k_ref[0, :, col:col + GDN_HEAD_DIM] = kn.astype(BF16)
                    knt = kn.T.astype(BF16)
                    for cc in range(nchunk):
                        kt_ref[0, cc, col:col + GDN_HEAD_DIM, :] = knt[:, cc * CHUNK:(cc + 1) * CHUNK]
        else:
            col = lo - 2 * GDN_QK_WIDTH
            v_ref[0, :, col:col + CONV_GROUP] = y.astype(BF16)
    cbuf_ref[0:tail, :] = cbuf_ref[tm:tm + tail, :]

    for c in range(GDN_V_WIDTH // CONV_GROUP):
        lo = c * CONV_GROUP
        z = _dot(hb, w_ref[:, GDN_CONV_CH + lo:GDN_CONV_CH + lo + CONV_GROUP])
        z_ref[0, :, lo:lo + CONV_GROUP] = z.astype(BF16)

    nh = GDN_V_HEADS
    ba = _dot(hb, wba_ref[...])
    beta = _sigmoid(ba[:, 0:nh])
    g = -jnp.exp(alog_ref[...]) * _softplus(ba[:, LANES:LANES + nh] + dtb_ref[...])
    row = lax.broadcasted_iota(jnp.int32, (tm, tm), 0)
    col = lax.broadcasted_iota(jnp.int32, (tm, tm), 1)
    same = (row // CHUNK) == (col // CHUNK)
    tri_lo = jnp.where(same & (col <= row), 1.0, 0.0).astype(BF16)
    tri_up = jnp.where(same & (row <= col), 1.0, 0.0).astype(BF16)
    gc = sum(_dot(tri_lo, p) for p in _split3(g))
    bg_ref[0, :, 0:nh] = beta
    bg_ref[0, :, nh:2 * nh] = gc
    bat = _dot_nt(wbat_ref[...], hb)
    g_t = -jnp.exp(alogc_ref[...]) * _softplus(bat + dtbc_ref[...])
    gct = sum(_dot(p, tri_up) for p in _split3(g_t))
    for cc in range(nchunk):
        gt_ref[0, cc] = gct[:, cc * CHUNK:(cc + 1) * CHUNK]


def _gdn_in(x, mod, nw, w_main, w_ba, w_at, conv_w, a_log, dt_bias):
    B, S, D = x.shape
    tm = TM_GDN_IN
    nh = GDN_V_HEADS
    kern = functools.partial(_gdn_in_kernel, tm=tm)
    tok = lambda width: pl.BlockSpec((1, tm, width), lambda b, s: (b, s, 0))
    return pl.pallas_call(
        kern,
        grid=(B, S // tm),
        in_specs=[
            tok(D),
            pl.BlockSpec((1, 1, 3, D), lambda b, s: (b, 0, 0, 0)),
            _resident((1, D)),
            _resident(w_main.shape),
            _resident(w_ba.shape),
            _resident(w_at.shape),
            _resident(conv_w.shape),
            _resident((1, nh)), _resident((1, nh)), _resident((nh, 1)), _resident((nh, 1)),
        ],
        out_specs=[
            tok(GDN_QK_WIDTH), tok(GDN_QK_WIDTH),
            pl.BlockSpec((1, tm // CHUNK, GDN_QK_WIDTH, CHUNK), lambda b, s: (b, s, 0, 0)),
            tok(GDN_V_WIDTH), tok(GDN_V_WIDTH),
            tok(2 * nh),
            pl.BlockSpec((1, tm // CHUNK, nh, CHUNK), lambda b, s: (b, s, 0, 0)),
        ],
        out_shape=[
            jax.ShapeDtypeStruct((B, S, GDN_QK_WIDTH), BF16),
            jax.ShapeDtypeStruct((B, S, GDN_QK_WIDTH), BF16),
            jax.ShapeDtypeStruct((B, S // CHUNK, GDN_QK_WIDTH, CHUNK), BF16),
            jax.ShapeDtypeStruct((B, S, GDN_V_WIDTH), BF16),
            jax.ShapeDtypeStruct((B, S, GDN_V_WIDTH), BF16),
            jax.ShapeDtypeStruct((B, S, 2 * nh), F32),
            jax.ShapeDtypeStruct((B, S // CHUNK, nh, CHUNK), F32),
        ],
        scratch_shapes=[pltpu.VMEM((tm + SUBLANES, GDN_CONV_CH), F32)],
        compiler_params=_params("arbitrary", "arbitrary"),
        name="gdn_in",
    )(x, mod.reshape(B, 1, 3, D), nw.reshape(1, D), w_main, w_ba, w_at, conv_w,
      a_log.reshape(1, nh), dt_bias.reshape(1, nh), a_log.reshape(nh, 1), dt_bias.reshape(nh, 1))


def _delta_kernel(q_ref, k_ref, kt_ref, v_ref, z_ref, bg_ref, gt_ref, nw_ref, o_ref, s_ref, *, tc):
    s = pl.program_id(1)
    C = CHUNK
    dh = GDN_HEAD_DIM
    nqk = GDN_QK_HEADS
    nv = GDN_V_HEADS
    rep = nv // nqk
    heads = range(nv)

    @pl.when(s == 0)
    def _():
        s_ref[...] = jnp.zeros(s_ref.shape, F32)

    row = lax.broadcasted_iota(jnp.int32, (C, C), 0)
    col = lax.broadcasted_iota(jnp.int32, (C, C), 1)
    lower = row >= col
    strict = row > col
    eye = jnp.where(row == col, 1.0, 0.0).astype(F32)
    nw = nw_ref[...]

    nper = DELTA_CHUNKS_PER_TRIP
    units = [(t, h) for t in range(nper) for h in heads]

    def trip(i, _):
        cs = [i * nper + t for t in range(nper)]
        rows = [pl.ds(pl.multiple_of(c * C, C), C) for c in cs]
        bg = [bg_ref[0, rows[t], :] for t in range(nper)]
        gt = [gt_ref[0, cs[t]] for t in range(nper)]
        groups = [(t, j) for t in range(nper) for j in range(nqk)]
        qb = {(t, j): q_ref[0, rows[t], j * dh:(j + 1) * dh] for t, j in groups}
        kb = {(t, j): k_ref[0, rows[t], j * dh:(j + 1) * dh] for t, j in groups}
        ktb = {(t, j): kt_ref[0, cs[t], j * dh:(j + 1) * dh, :] for t, j in groups}
        kq = {g: _dot(jnp.concatenate([kb[g], qb[g]], axis=0), ktb[g]) for g in groups}
        grp = {(t, h): (t, h // rep) for t, h in units}

        beta = {(t, h): bg[t][:, h:h + 1] for t, h in units}
        gc = {(t, h): bg[t][:, nv + h:nv + h + 1] for t, h in units}
        gr = {(t, h): gt[t][h:h + 1, :] for t, h in units}
        g_last = {u: gr[u][:, C - 1:C] for u in units}
        decay = {u: jnp.exp(jnp.where(lower, gc[u] - gr[u], -jnp.inf)) for u in units}
        egc = {u: jnp.exp(gc[u]) for u in units}

        X = {u: -(jnp.where(strict, kq[grp[u]][:C] * decay[u], 0.0) * beta[u]) for u in units}
        negL = X
        P = {u: eye + X[u] for u in units}
        Xb = {u: X[u].astype(BF16) for u in units}
        X = {u: _dot(Xb[u], Xb[u]) for u in units}
        p = 4
        while p < C // 2:
            Xb = {u: X[u].astype(BF16) for u in units}
            PX = {u: _dot(jnp.concatenate([P[u], X[u]], axis=0).astype(BF16), Xb[u]) for u in units}
            P = {u: P[u] + PX[u][:C] for u in units}
            X = {u: PX[u][C:] for u in units}
            p *= 2
        P = {u: P[u] + _dot(P[u].astype(BF16), X[u].astype(BF16)) for u in units}
        Tb, rest = {}, {}
        for u in units:
            Tb[u] = P[u].astype(BF16)
            nl_hi = negL[u].astype(BF16)
            nl_lo = (negL[u] - nl_hi.astype(F32)).astype(BF16)
            hl = _dot(jnp.concatenate([nl_hi, nl_lo], axis=0), Tb[u])
            rest[u] = (eye - Tb[u].astype(F32)) + (hl[:C] + hl[C:])
        P = {u: Tb[u].astype(F32) + _dot(Tb[u], rest[u].astype(BF16)) for u in units}

        uw = {}
        for t, h in units:
            u = (t, h)
            vf = v_ref[0, rows[t], h * dh:(h + 1) * dh].astype(F32)
            kf = kb[grp[u]].astype(F32)
            rhs = jnp.concatenate([vf * beta[u], kf * (beta[u] * egc[u])], axis=1).astype(BF16)
            uw[u] = _dot(P[u].astype(BF16), rhs)
        wq = {u: jnp.concatenate([uw[u][:, dh:], qb[grp[u]].astype(F32) * egc[u]], axis=0).astype(BF16)
              for u in units}
        ak_lhs = {}
        for u in units:
            attn = (kq[grp[u]][C:] * decay[u]).astype(BF16)
            kdt = (ktb[grp[u]].astype(F32) * jnp.exp(g_last[u] - gr[u])).astype(BF16)
            ak_lhs[u] = jnp.concatenate([attn, kdt], axis=0)
        for t in range(nper):
            S = [s_ref[h] for h in heads]
            r = [_dot(wq[(t, h)], S[h].astype(BF16)) for h in heads]
            vnb = [(uw[(t, h)][:, :dh] - r[h][:C]).astype(BF16) for h in heads]
            ak = [_dot(ak_lhs[(t, h)], vnb[h]) for h in heads]
            for h in heads:
                s_ref[h] = S[h] * jnp.exp(g_last[(t, h)]) + ak[h][C:]
            for h in heads:
                o = r[h][C:] + ak[h][:C]
                var = jnp.mean(o * o, axis=-1, keepdims=True)
                zf = z_ref[0, rows[t], h * dh:(h + 1) * dh].astype(F32)
                o_ref[0, rows[t], h * dh:(h + 1) * dh] = (o * lax.rsqrt(var + EPS) * nw * _silu(zf)).astype(BF16)
        return 0

    lax.fori_loop(0, tc // (C * nper), trip, 0)


def _gdn_delta(q, k, kt, v, z, bg, gt, nw):
    B, S, _ = q.shape
    tc = TC_DELTA
    nv = GDN_V_HEADS
    dh = GDN_HEAD_DIM
    kern = functools.partial(_delta_kernel, tc=tc)
    tok = lambda width: pl.BlockSpec((1, tc, width), lambda b, s: (b, s, 0))
    return pl.pallas_call(
        kern,
        grid=(B, S // tc),
        in_specs=[
            tok(GDN_QK_WIDTH), tok(GDN_QK_WIDTH),
            pl.BlockSpec((1, tc // CHUNK, GDN_QK_WIDTH, CHUNK), lambda b, s: (b, s, 0, 0)),
            tok(GDN_V_WIDTH), tok(GDN_V_WIDTH),
            tok(2 * nv),
            pl.BlockSpec((1, tc // CHUNK, nv, CHUNK), lambda b, s: (b, s, 0, 0)),
            pl.BlockSpec((1, dh), lambda b, s: (0, 0)),
        ],
        out_specs=tok(GDN_V_WIDTH),
        out_shape=jax.ShapeDtypeStruct((B, S, GDN_V_WIDTH), BF16),
        scratch_shapes=[pltpu.VMEM((nv, dh, dh), F32)],
        compiler_params=_params("arbitrary", "arbitrary"),
        name="gdn_delta",
    )(q, k, kt, v, z, bg, gt, nw.reshape(1, dh))


def _fox_in_kernel(x_ref, mod_ref, nw_ref, wt_ref, wz_ref, wft_ref, fb_ref, qw_ref, kw_ref,
                   qt_ref, kt_ref, vt_ref, z_ref, cum_ref, carry_ref, *, tm):
    s = pl.program_id(1)
    W = FOX_WIDTH
    dh = FOX_HEAD_DIM

    @pl.when(s == 0)
    def _():
        carry_ref[...] = jnp.zeros(carry_ref.shape, F32)

    h = _modulated_norm(x_ref[0], mod_ref[0, 0], nw_ref[...])
    hb = h.astype(BF16)

    grp = FOX_GROUP
    for g in range(3 * W // grp):
        blk = _dot_nt(wt_ref[g * grp:(g + 1) * grp, :], hb)
        r0 = (g * grp) % W
        if g * grp < 2 * W:
            is_q = g * grp < W
            wcol = qw_ref[...] if is_q else kw_ref[...]
            dst = qt_ref if is_q else kt_ref
            for hh in range(grp // dh):
                seg = blk[hh * dh:(hh + 1) * dh]
                inv = lax.rsqrt(jnp.mean(seg * seg, axis=0, keepdims=True) + EPS)
                dst[0, r0 + hh * dh:r0 + (hh + 1) * dh, :] = (seg * inv * wcol).astype(BF16)
        else:
            vt_ref[0, r0:r0 + grp, :] = blk.astype(BF16)
    z_ref[0] = _dot(hb, wz_ref[...]).astype(BF16)

    xf = _dot_nt(wft_ref[...], hb) + fb_ref[...]
    log_f = -_softplus(-xf)
    row = lax.broadcasted_iota(jnp.int32, (tm, tm), 0)
    col = lax.broadcasted_iota(jnp.int32, (tm, tm), 1)
    tri_up = jnp.where(row <= col, 1.0, 0.0).astype(BF16)
    cum = sum(_dot(p, tri_up) for p in _split3(log_f)) + carry_ref[...]
    cum_ref[0] = cum
    carry_ref[...] = cum[:, tm - 1:tm]


def _fox_in(x, mod, nw, w_qkv_t, w_z, w_ft, f_bias, qw, kw):
    B, S, D = x.shape
    tm = TM_PROJ
    H = FOX_HEADS
    dh = FOX_HEAD_DIM
    W = FOX_WIDTH
    kern = functools.partial(_fox_in_kernel, tm=tm)
    feat_major = pl.BlockSpec((1, W, tm), lambda b, s: (b, 0, s))
    fm_shape = jax.ShapeDtypeStruct((B, W, S), BF16)
    return pl.pallas_call(
        kern,
        grid=(B, S // tm),
        in_specs=[
            pl.BlockSpec((1, tm, D), lambda b, s: (b, s, 0)),
            pl.BlockSpec((1, 1, 3, D), lambda b, s: (b, 0, 0, 0)),
            _resident((1, D)),
            _resident(w_qkv_t.shape),
            _resident(w_z.shape),
            _resident(w_ft.shape),
            _resident((H, 1)),
            _resident((dh, 1)),
            _resident((dh, 1)),
        ],
        out_specs=[
            feat_major, feat_major, feat_major,
            pl.BlockSpec((1, tm, W), lambda b, s: (b, s, 0)),
            pl.BlockSpec((1, H, tm), lambda b, s: (b, 0, s)),
        ],
        out_shape=[
            fm_shape, fm_shape, fm_shape,
            jax.ShapeDtypeStruct((B, S, W), BF16),
            jax.ShapeDtypeStruct((B, H, S), F32),
        ],
        scratch_shapes=[pltpu.VMEM((H, 1), F32)],
        compiler_params=_params("arbitrary", "arbitrary"),
        name="fox_in",
    )(x, mod.reshape(B, 1, 3, D), nw.reshape(1, D), w_qkv_t, w_z, w_ft, f_bias.reshape(H, 1),
      qw.reshape(dh, 1), kw.reshape(dh, 1))


def _fox_attn_kernel(qt_ref, kt_ref, vt_ref, z_ref, cum_ref, o_ref,
                     qa_ref, ka_ref, va_ref, r_ref, m_ref, acc_ref, st_ref, p_ref, *, tq, seq):
    hp = pl.program_id(1)
    dh = FOX_HEAD_DIM
    pair = LANES // dh
    nq = seq // tq
    nsplit = 3
    heads = range(pair)
    kidx = lax.broadcasted_iota(jnp.int32, (tq, tq), 0)
    qidx = lax.broadcasted_iota(jnp.int32, (tq, tq), 1)
    causal = kidx <= qidx

    aug_row = lax.broadcasted_iota(jnp.int32, (BF16_ROWS, seq), 0)
    pick = jnp.where(aug_row < nsplit, 1.0, 0.0).astype(BF16)
    one_row = jnp.where(aug_row < 1, 1.0, 0.0).astype(BF16)
    for e in heads:
        qa_ref[e, 0:dh, :] = qt_ref[0, e * dh:(e + 1) * dh, :]
        qa_ref[e, dh:dh + BF16_ROWS, :] = pick
        qa_ref[e, dh + BF16_ROWS:, :] = jnp.zeros((LANES - dh - BF16_ROWS, seq), BF16)
        va_ref[e, 0:dh, :] = vt_ref[0, e * dh:(e + 1) * dh, :]
        va_ref[e, dh:, :] = one_row
        crow = cum_ref[0, pl.ds(pair * hp + e, 1), :]
        firsts = [jnp.broadcast_to(crow[:, j * tq:j * tq + 1], (1, tq)) for j in range(nq)]
        for j in range(nq):
            r_ref[e, j] = firsts[j] * (-LOG2E)
        rel = (crow - jnp.concatenate(firsts, axis=1)) * (-LOG2E)
        parts = [p.astype(F32) for p in _split3(rel)]
        btile = jnp.concatenate(parts + [jnp.zeros((SUBLANES - nsplit, seq), F32)], axis=0)
        for blk in range(seq // KT_BLOCK):
            sl = slice(blk * KT_BLOCK, (blk + 1) * KT_BLOCK)
            top = jnp.concatenate([kt_ref[0, e * dh:(e + 1) * dh, sl].astype(F32), btile[:, sl],
                                   jnp.zeros((LANES - dh - SUBLANES, KT_BLOCK), F32)], axis=0)
            ka_ref[e, sl, :] = top.T.astype(BF16)

    m_ref[...] = jnp.full(m_ref.shape, -jnp.inf, F32)
    acc_ref[...] = jnp.zeros(acc_ref.shape, F32)
    p_ref[...] = jnp.zeros(p_ref.shape, BF16)

    def scores(e, qi, kj):
        return _dot(ka_ref[e, pl.ds(pl.multiple_of(kj * tq, tq), tq), :],
                    qa_ref[e, :, pl.ds(pl.multiple_of(qi * tq, tq), tq)])

    def weighted_values(e, kj, p):
        return _dot(va_ref[e, :, pl.ds(pl.multiple_of(kj * tq, tq), tq)], p)

    def softmax_update(e, qi, kj, read_scores):
        off = r_ref[e, kj]
        m_rel = m_ref[e, qi] - off
        m_new = jnp.maximum(m_rel, jnp.max(read_scores(), axis=0, keepdims=True))
        m_ref[e, qi] = m_new + off
        return jnp.exp2(m_rel - m_new), jnp.exp2(read_scores() - m_new).astype(BF16)

    def mask_diagonal(st, masked):
        return jnp.where(causal, st, -jnp.inf) if masked else st

    def simple_step(pr, masked):
        qi, kj = pr
        for e in heads:
            st = mask_diagonal(scores(e, qi, kj), masked)
            alpha, p = softmax_update(e, qi, kj, lambda st=st: st)
            acc_ref[e, qi] = alpha * acc_ref[e, qi] + weighted_values(e, kj, p)

    def sweep(first, advance, npairs, masked, unroll):
        n_main = (npairs // unroll) * unroll
        cur = first
        if n_main:
            second = advance(*first)
            for e in heads:
                st_ref[e, 0] = scores(e, *first)
                st_ref[e, 1] = scores(e, *second)

            def body(_, carry):
                prev, cur, nxt, a_prev = carry
                for i in range(unroll):
                    nxt2 = advance(*nxt)
                    new_a = []
                    for e in heads:
                        pv_prev = weighted_values(e, prev[1], p_ref[e, (i - 1) % P_RING])
                        st_ref[e, (i + 2) % ST_RING] = scores(e, *nxt2)
                        alpha, p = softmax_update(
                            e, cur[0], cur[1], lambda e=e, i=i: mask_diagonal(st_ref[e, i % ST_RING], masked))
                        p_ref[e, i % P_RING] = p
                        acc_ref[e, prev[0]] = a_prev[e] * acc_ref[e, prev[0]] + pv_prev
                        new_a.append(alpha)
                    prev, cur, nxt, a_prev = cur, nxt, nxt2, tuple(new_a)
                return prev, cur, nxt, a_prev

            ones = tuple(jnp.ones((1, tq), F32) for _ in heads)
            spare = (jnp.int32(nq), jnp.int32(0))
            prev, cur, _, a_prev = lax.fori_loop(0, n_main // unroll, body, (spare, first, second, ones))
            for e in heads:
                pv_prev = weighted_values(e, prev[1], p_ref[e, (unroll - 1) % P_RING])
                acc_ref[e, prev[0]] = a_prev[e] * acc_ref[e, prev[0]] + pv_prev

        def tail(_, pr):
            simple_step(pr, masked)
            return advance(*pr)

        lax.fori_loop(0, npairs - n_main, tail, cur)

    def next_off_diagonal(qi, kj):
        k2 = kj + 1
        wrap = k2 >= qi
        return jnp.where(wrap, jnp.minimum(qi + 1, nq - 1), qi), jnp.where(wrap, 0, k2)

    def next_diagonal(qi, kj):
        nxt = jnp.minimum(qi + 1, nq - 1)
        return nxt, nxt

    zero = jnp.int32(0)
    sweep((jnp.int32(min(1, nq - 1)), zero), next_off_diagonal, nq * (nq - 1) // 2, False, ATTN_UNROLL)
    sweep((zero, zero), next_diagonal, nq, True, ATTN_UNROLL_DIAG)

    def finish(qi, _):
        accs = [acc_ref[e, qi] for e in heads]
        o = jnp.concatenate([a[0:dh] * (1.0 / a[dh:dh + 1]) for a in accs], axis=0).T
        rows = pl.ds(pl.multiple_of(qi * tq, tq), tq)
        zf = z_ref[0, rows, :].astype(F32)
        o_ref[0, rows, :] = (o * _silu(zf)).astype(BF16)
        return 0

    lax.fori_loop(0, nq, finish, 0)


def _fox_attn(qt, kt, vt, z, cum):
    B, W, S = qt.shape
    H = FOX_HEADS
    dh = FOX_HEAD_DIM
    pair = LANES // dh
    tq = TQ_ATTN
    nq = S // tq
    kern = functools.partial(_fox_attn_kernel, tq=tq, seq=S)
    fm = pl.BlockSpec((1, LANES, S), lambda b, h: (b, h, 0))
    return pl.pallas_call(
        kern,
        grid=(B, H // pair),
        in_specs=[
            fm, fm, fm,
            pl.BlockSpec((1, S, LANES), lambda b, h: (b, 0, h)),
            pl.BlockSpec((1, H, S), lambda b, h: (b, 0, 0)),
        ],
        out_specs=pl.BlockSpec((1, S, LANES), lambda b, h: (b, 0, h)),
        out_shape=jax.ShapeDtypeStruct((B, S, W), BF16),
        scratch_shapes=[
            pltpu.VMEM((pair, LANES, S), BF16),
            pltpu.VMEM((pair, S, LANES), BF16),
            pltpu.VMEM((pair, dh + BF16_ROWS, S), BF16),
            pltpu.VMEM((pair, nq, 1, tq), F32),
            pltpu.VMEM((pair, nq, 1, tq), F32),
            pltpu.VMEM((pair, nq + 1, dh + BF16_ROWS, tq), F32),
            pltpu.VMEM((pair, ST_RING, tq, tq), F32),
            pltpu.VMEM((pair, P_RING, tq, tq), BF16),
        ],
        compiler_params=_params("arbitrary", "arbitrary"),
        name="fox_attn",
    )(qt, kt, vt, z, cum)


def _out_kernel(o_ref, x_ref, mod_ref, w_ref, fnw_ref, y_ref, *, final):
    y = _dot(o_ref[0], w_ref[...])
    xn = x_ref[0] + mod_ref[0, 0][2:3, :] * y
    if final:
        ms = jnp.mean(xn * xn, axis=-1, keepdims=True)
        xn = xn * lax.rsqrt(ms + EPS) * fnw_ref[...]
    y_ref[0] = xn


def _out_proj(o, x, mod, w_out, fnw, final):
    B, S, D = x.shape
    width = o.shape[-1]
    tm = TM_PROJ
    kern = functools.partial(_out_kernel, final=final)
    return pl.pallas_call(
        kern,
        grid=(B, S // tm),
        in_specs=[
            pl.BlockSpec((1, tm, width), lambda b, s: (b, s, 0)),
            pl.BlockSpec((1, tm, D), lambda b, s: (b, s, 0)),
            pl.BlockSpec((1, 1, 3, D), lambda b, s: (b, 0, 0, 0)),
            _resident(w_out.shape),
            _resident((1, D)),
        ],
        out_specs=pl.BlockSpec((1, tm, D), lambda b, s: (b, s, 0)),
        out_shape=jax.ShapeDtypeStruct((B, S, D), F32),
        compiler_params=_params("arbitrary", "arbitrary"),
        name="out_proj",
    )(o, x, mod.reshape(B, 1, 3, D), w_out, fnw.reshape(1, D))


def kernel(x, c, norm_w, ada_w, ada_b, a_w_in, a_conv_w, a_A_log, a_dt_bias, a_norm_w, a_w_out,
           b_w_in, b_f_bias, b_qn_w, b_kn_w, b_w_out, final_norm_w):
    B, S, D = x.shape
    assert D == D_MODEL and S % max(TM_PROJ, TM_GDN_IN, TC_DELTA, TQ_ATTN) == 0
    mods = _ada_mod(c, ada_w, ada_b)
    nh = GDN_V_HEADS
    for i in range(DEPTH):
        j = i // 2
        final = i == DEPTH - 1
        if i % 2 == 0:
            w_in = a_w_in[j]
            w_main = w_in[:, :GDN_CONV_CH + GDN_V_WIDTH].astype(BF16)
            w_b = w_in[:, GDN_CONV_CH + GDN_V_WIDTH:GDN_CONV_CH + GDN_V_WIDTH + nh]
            w_a = w_in[:, GDN_CONV_CH + GDN_V_WIDTH + nh:]
            pad = jnp.zeros((D, LANES - nh), F32)
            w_ba = jnp.concatenate([w_b, pad, w_a, pad], axis=1).astype(BF16)
            w_at = w_a.T.astype(BF16)
            q, k, kt, v, z, bg, gt = _gdn_in(x, mods[i], norm_w[i], w_main, w_ba, w_at, a_conv_w[j],
                                              a_A_log[j], a_dt_bias[j])
            o = _gdn_delta(q, k, kt, v, z, bg, gt, a_norm_w[j])
            w_out = a_w_out[j].astype(BF16)
        else:
            w_in = b_w_in[j]
            w_qkv_t = w_in[:, :3 * FOX_WIDTH].T.astype(BF16)
            w_z = w_in[:, 3 * FOX_WIDTH:4 * FOX_WIDTH].astype(BF16)
            w_ft = w_in[:, 4 * FOX_WIDTH:].T.astype(BF16)
            qw = b_qn_w[j] * (FOX_HEAD_DIM ** -0.5 * LOG2E)
            qt, kt, vt, z, cum = _fox_in(x, mods[i], norm_w[i], w_qkv_t, w_z, w_ft, b_f_bias[j], qw, b_kn_w[j])
            o = _fox_attn(qt, kt, vt, z, cum)
            w_out = b_w_out[j].astype(BF16)
        x = _out_proj(o, x, mods[i], w_out, final_norm_w, final)
    return x
```

```python
import functools

import jax
import jax.numpy as jnp
from jax import lax
from jax.experimental import pallas as pl
from jax.experimental.pallas import tpu as pltpu

F32 = jnp.float32
BF16 = jnp.bfloat16

D_MODEL = 1024
DEPTH = 4
EPS = 1e-6
CHUNK = 64

GDN_QK_HEADS = 8
GDN_V_HEADS = 16
GDN_HEAD_DIM = 128
GDN_QK_WIDTH = GDN_QK_HEADS * GDN_HEAD_DIM
GDN_V_WIDTH = GDN_V_HEADS * GDN_HEAD_DIM
GDN_CONV_CH = 2 * GDN_QK_WIDTH + GDN_V_WIDTH
CONV_WIDTH = 4

FOX_HEADS = 16
FOX_HEAD_DIM = 64
FOX_WIDTH = FOX_HEADS * FOX_HEAD_DIM

LANES = 128
SUBLANES = 8
BF16_ROWS = 16
LOG2E = 1.4426950408889634
VMEM_LIMIT = 48 * 1024 * 1024

TM_PROJ = 512
TM_GDN_IN = 512
TC_DELTA = 512
DELTA_CHUNKS_PER_TRIP = 2
TQ_ATTN = 256
CONV_GROUP = 512
FOX_GROUP = 512
KT_BLOCK = 512
ATTN_UNROLL = 40
ATTN_UNROLL_DIAG = 16
ST_RING = 4
P_RING = 2


def _sigmoid(x):
    return 1.0 / (1.0 + jnp.exp(-x))


def _silu(x):
    return x * _sigmoid(x)


def _softplus(x):
    return jnp.maximum(x, 0.0) + jnp.log(1.0 + jnp.exp(-jnp.abs(x)))


def _split3(a):
    hi = a.astype(BF16)
    r = a - hi.astype(F32)
    mid = r.astype(BF16)
    lo = (r - mid.astype(F32)).astype(BF16)
    return hi, mid, lo


def _dot(a, b):
    return jnp.dot(a, b, preferred_element_type=F32)


def _dot_nt(a, b):
    return lax.dot_general(a, b, (((1,), (1,)), ((), ())), preferred_element_type=F32)


def _dot_tn(a, b):
    return lax.dot_general(a, b, (((0,), (0,)), ((), ())), preferred_element_type=F32)


def _modulated_norm(x, mod, nw):
    ms = jnp.mean(x * x, axis=-1, keepdims=True)
    y = x * lax.rsqrt(ms + EPS) * nw
    return y * (1.0 + mod[1:2, :]) + mod[0:1, :]


def _params(*sem):
    return pltpu.CompilerParams(dimension_semantics=sem, vmem_limit_bytes=VMEM_LIMIT)


def _resident(shape):
    nd = len(shape)
    return pl.BlockSpec(shape, lambda *_: (0,) * nd, pipeline_mode=pl.Buffered(1))


def _ada_kernel(c_ref, w_ref, b_ref, o_ref):
    cond = _silu(c_ref[...])
    o_ref[0] = _dot(cond, w_ref[0]) + b_ref[0]


def _ada_mod(c, ada_w, ada_b):
    B, D = c.shape
    depth = ada_w.shape[0]
    out = pl.pallas_call(
        _ada_kernel,
        grid=(depth,),
        in_specs=[
            pl.BlockSpec((B, D), lambda i: (0, 0)),
            pl.BlockSpec((1, D, 3 * D), lambda i: (i, 0, 0)),
            pl.BlockSpec((1, 1, 3 * D), lambda i: (i, 0, 0)),
        ],
        out_specs=pl.BlockSpec((1, B, 3 * D), lambda i: (i, 0, 0)),
        out_shape=jax.ShapeDtypeStruct((depth, B, 3 * D), F32),
        compiler_params=_params("arbitrary"),
        name="ada_mod",
    )(c, ada_w, ada_b.reshape(depth, 1, 3 * D))
    return out.reshape(depth, B, 3, D)


def _gdn_in_kernel(x_ref, mod_ref, nw_ref, w_ref, wba_ref, wbat_ref, cw_ref,
                   alog_ref, dtb_ref, alogc_ref, dtbc_ref,
                   q_ref, k_ref, kt_ref, v_ref, z_ref, bg_ref, gt_ref, cbuf_ref, *, tm):
    s = pl.program_id(1)
    tail = SUBLANES
    nchunk = tm // CHUNK

    @pl.when(s == 0)
    def _():
        cbuf_ref[0:tail, :] = jnp.zeros((tail, GDN_CONV_CH), F32)

    h = _modulated_norm(x_ref[0], mod_ref[0, 0], nw_ref[...])
    hb = h.astype(BF16)

    qscale = GDN_HEAD_DIM ** -0.5
    for c in range(GDN_CONV_CH // CONV_GROUP):
        lo = c * CONV_GROUP
        pre = _dot(hb, w_ref[:, lo:lo + CONV_GROUP])
        cbuf_ref[tail:tail + tm, lo:lo + CONV_GROUP] = pre
        acc = cw_ref[CONV_WIDTH - 1:CONV_WIDTH, lo:lo + CONV_GROUP] * pre
        for j in range(CONV_WIDTH - 1):
            off = tail - (CONV_WIDTH - 1) + j
            acc = acc + cw_ref[j:j + 1, lo:lo + CONV_GROUP] * cbuf_ref[off:off + tm, lo:lo + CONV_GROUP]
        y = _silu(acc)
        if lo < 2 * GDN_QK_WIDTH:
            for hh in range(CONV_GROUP // GDN_HEAD_DIM):
                seg = y[:, hh * GDN_HEAD_DIM:(hh + 1) * GDN_HEAD_DIM]
                inv = lax.rsqrt(jnp.sum(seg * seg, axis=-1, keepdims=True) + EPS)
                col = lo + hh * GDN_HEAD_DIM
                if col < GDN_QK_WIDTH:
                    q_ref[0, :, col:col + GDN_HEAD_DIM] = (seg * inv * qscale).astype(BF16)
                else:
                    col -= GDN_QK_WIDTH
                    kn = seg * inv
                    k_ref[0, :, col:col + GDN_HEAD_DIM] = kn.astype(BF16)
                    knt = kn.T.astype(BF16)
                    for cc in range(nchunk):
                        kt_ref[0, cc, col:col + GDN_HEAD_DIM, :] = knt[:, cc * CHUNK:(cc + 1) * CHUNK]
        else:
            col = lo - 2 * GDN_QK_WIDTH
            v_ref[0, :, col:col + CONV_GROUP] = y.astype(BF16)
    cbuf_ref[0:tail, :] = cbuf_ref[tm:tm + tail, :]

    for c in range(GDN_V_WIDTH // CONV_GROUP):
        lo = c * CONV_GROUP
        z = _dot(hb, w_ref[:, GDN_CONV_CH + lo:GDN_CONV_CH + lo + CONV_GROUP])
        z_ref[0, :, lo:lo + CONV_GROUP] = z.astype(BF16)

    nh = GDN_V_HEADS
    ba = _dot(hb, wba_ref[...])
    beta = _sigmoid(ba[:, 0:nh])
    g = -jnp.exp(alog_ref[...]) * _softplus(ba[:, LANES:LANES + nh] + dtb_ref[...])
    row = lax.broadcasted_iota(jnp.int32, (tm, tm), 0)
    col = lax.broadcasted_iota(jnp.int32, (tm, tm), 1)
    same = (row // CHUNK) == (col // CHUNK)
    tri_lo = jnp.where(same & (col <= row), 1.0, 0.0).astype(BF16)
    tri_up = jnp.where(same & (row <= col), 1.0, 0.0).astype(BF16)
    gc = sum(_dot(tri_lo, p) for p in _split3(g))
    bg_ref[0, :, 0:nh] = beta
    bg_ref[0, :, nh:2 * nh] = gc
    bat = _dot_nt(wbat_ref[...], hb)
    g_t = -jnp.exp(alogc_ref[...]) * _softplus(bat + dtbc_ref[...])
    gct = sum(_dot(p, tri_up) for p in _split3(g_t))
    for cc in range(nchunk):
        gt_ref[0, cc] = gct[:, cc * CHUNK:(cc + 1) * CHUNK]


def _gdn_in(x, mod, nw, w_main, w_ba, w_at, conv_w, a_log, dt_bias):
    B, S, D = x.shape
    tm = TM_GDN_IN
    nh = GDN_V_HEADS
    kern = functools.partial(_gdn_in_kernel, tm=tm)
    tok = lambda width: pl.BlockSpec((1, tm, width), lambda b, s: (b, s, 0))
    return pl.pallas_call(
        kern,
        grid=(B, S // tm),
        in_specs=[
            tok(D),
            pl.BlockSpec((1, 1, 3, D), lambda b, s: (b, 0, 0, 0)),
            _resident((1, D)),
            _resident(w_main.shape),
            _resident(w_ba.shape),
            _resident(w_at.shape),
            _resident(conv_w.shape),
            _resident((1, nh)), _resident((1, nh)), _resident((nh, 1)), _resident((nh, 1)),
        ],
        out_specs=[
            tok(GDN_QK_WIDTH), tok(GDN_QK_WIDTH),
            pl.BlockSpec((1, tm // CHUNK, GDN_QK_WIDTH, CHUNK), lambda b, s: (b, s, 0, 0)),
            tok(GDN_V_WIDTH), tok(GDN_V_WIDTH),
            tok(2 * nh),
            pl.BlockSpec((1, tm // CHUNK, nh, CHUNK), lambda b, s: (b, s, 0, 0)),
        ],
        out_shape=[
            jax.ShapeDtypeStruct((B, S, GDN_QK_WIDTH), BF16),
            jax.ShapeDtypeStruct((B, S, GDN_QK_WIDTH), BF16),
            jax.ShapeDtypeStruct((B, S // CHUNK, GDN_QK_WIDTH, CHUNK), BF16),
            jax.ShapeDtypeStruct((B, S, GDN_V_WIDTH), BF16),
            jax.ShapeDtypeStruct((B, S, GDN_V_WIDTH), BF16),
            jax.ShapeDtypeStruct((B, S, 2 * nh), F32),
            jax.ShapeDtypeStruct((B, S // CHUNK, nh, CHUNK), F32),
        ],
        scratch_shapes=[pltpu.VMEM((tm + SUBLANES, GDN_CONV_CH), F32)],
        compiler_params=_params("arbitrary", "arbitrary"),
        name="gdn_in",
    )(x, mod.reshape(B, 1, 3, D), nw.reshape(1, D), w_main, w_ba, w_at, conv_w,
      a_log.reshape(1, nh), dt_bias.reshape(1, nh), a_log.reshape(nh, 1), dt_bias.reshape(nh, 1))


def _delta_kernel(q_ref, k_ref, kt_ref, v_ref, z_ref, bg_ref, gt_ref, nw_ref, o_ref, s_ref, *, tc):
    s = pl.program_id(1)
    C = CHUNK
    dh = GDN_HEAD_DIM
    nqk = GDN_QK_HEADS
    nv = GDN_V_HEADS
    rep = nv // nqk
    heads = range(nv)

    @pl.when(s == 0)
    def _():
        s_ref[...] = jnp.zeros(s_ref.shape, F32)

    row = lax.broadcasted_iota(jnp.int32, (C, C), 0)
    col = lax.broadcasted_iota(jnp.int32, (C, C), 1)
    lower = row >= col
    strict = row > col
    eye = jnp.where(row == col, 1.0, 0.0).astype(F32)
    nw = nw_ref[...]

    nper = DELTA_CHUNKS_PER_TRIP
    units = [(t, h) for t in range(nper) for h in heads]

    def trip(i, _):
        cs = [i * nper + t for t in range(nper)]
        rows = [pl.ds(pl.multiple_of(c * C, C), C) for c in cs]
        bg = [bg_ref[0, rows[t], :] for t in range(nper)]
        gt = [gt_ref[0, cs[t]] for t in range(nper)]
        groups = [(t, j) for t in range(nper) for j in range(nqk)]
        qb = {(t, j): q_ref[0, rows[t], j * dh:(j + 1) * dh] for t, j in groups}
        kb = {(t, j): k_ref[0, rows[t], j * dh:(j + 1) * dh] for t, j in groups}
        ktb = {(t, j): kt_ref[0, cs[t], j * dh:(j + 1) * dh, :] for t, j in groups}
        kq = {g: _dot(jnp.concatenate([kb[g], qb[g]], axis=0), ktb[g]) for g in groups}
        grp = {(t, h): (t, h // rep) for t, h in units}

        beta = {(t, h): bg[t][:, h:h + 1] for t, h in units}
        gc = {(t, h): bg[t][:, nv + h:nv + h + 1] for t, h in units}
        gr = {(t, h): gt[t][h:h + 1, :] for t, h in units}
        g_last = {u: gr[u][:, C - 1:C] for u in units}
        decay = {u: jnp.exp(jnp.where(lower, gc[u] - gr[u], -jnp.inf)) for u in units}
        egc = {u: jnp.exp(gc[u]) for u in units}

        X = {u: -(jnp.where(strict, kq[grp[u]][:C] * decay[u], 0.0) * beta[u]) for u in units}
        negL = X
        P = {u: eye + X[u] for u in units}
        Xb = {u: X[u].astype(BF16) for u in units}
        X = {u: _dot(Xb[u], Xb[u]) for u in units}
        p = 4
        while p < C // 2:
            Xb = {u: X[u].astype(BF16) for u in units}
            PX = {u: _dot(jnp.concatenate([P[u], X[u]], axis=0).astype(BF16), Xb[u]) for u in units}
            P = {u: P[u] + PX[u][:C] for u in units}
            X = {u: PX[u][C:] for u in units}
            p *= 2
        P = {u: P[u] + _dot(P[u].astype(BF16), X[u].astype(BF16)) for u in units}
        Tb, rest = {}, {}
        for u in units:
            Tb[u] = P[u].astype(BF16)
            nl_hi = negL[u].astype(BF16)
            nl_lo = (negL[u] - nl_hi.astype(F32)).astype(BF16)
            hl = _dot(jnp.concatenate([nl_hi, nl_lo], axis=0), Tb[u])
            rest[u] = (eye - Tb[u].astype(F32)) + (hl[:C] + hl[C:])
        P = {u: Tb[u].astype(F32) + _dot(Tb[u], rest[u].astype(BF16)) for u in units}

        uw = {}
        for t, h in units:
            u = (t, h)
            vf = v_ref[0, rows[t], h * dh:(h + 1) * dh].astype(F32)
            kf = kb[grp[u]].astype(F32)
            rhs = jnp.concatenate([vf * beta[u], kf * (beta[u] * egc[u])], axis=1).astype(BF16)
            uw[u] = _dot(P[u].astype(BF16), rhs)
        wq = {u: jnp.concatenate([uw[u][:, dh:], qb[grp[u]].astype(F32) * egc[u]], axis=0).astype(BF16)
              for u in units}
        ak_lhs = {}
        for u in units:
            attn = (kq[grp[u]][C:] * decay[u]).astype(BF16)
            kdt = (ktb[grp[u]].astype(F32) * jnp.exp(g_last[u] - gr[u])).astype(BF16)
            ak_lhs[u] = jnp.concatenate([attn, kdt], axis=0)
        for t in range(nper):
            S = [s_ref[h] for h in heads]
            r = [_dot(wq[(t, h)], S[h].astype(BF16)) for h in heads]
            vnb = [(uw[(t, h)][:, :dh] - r[h][:C]).astype(BF16) for h in heads]
            ak = [_dot(ak_lhs[(t, h)], vnb[h]) for h in heads]
            for h in heads:
                s_ref[h] = S[h] * jnp.exp(g_last[(t, h)]) + ak[h][C:]
            for h in heads:
                o = r[h][C:] + ak[h][:C]
                var = jnp.mean(o * o, axis=-1, keepdims=True)
                zf = z_ref[0, rows[t], h * dh:(h + 1) * dh].astype(F32)
                o_ref[0, rows[t], h * dh:(h + 1) * dh] = (o * lax.rsqrt(var + EPS) * nw * _silu(zf)).astype(BF16)
        return 0

    lax.fori_loop(0, tc // (C * nper), trip, 0)


def _gdn_delta(q, k, kt, v, z, bg, gt, nw):
    B, S, _ = q.shape
    tc = TC_DELTA
    nv = GDN_V_HEADS
    dh = GDN_HEAD_DIM
    kern = functools.partial(_delta_kernel, tc=tc)
    tok = lambda width: pl.BlockSpec((1, tc, width), lambda b, s: (b, s, 0))
    return pl.pallas_call(
        kern,
        grid=(B, S // tc),
        in_specs=[
            tok(GDN_QK_WIDTH), tok(GDN_QK_WIDTH),
            pl.BlockSpec((1, tc // CHUNK, GDN_QK_WIDTH, CHUNK), lambda b, s: (b, s, 0, 0)),
            tok(GDN_V_WIDTH), tok(GDN_V_WIDTH),
            tok(2 * nv),
            pl.BlockSpec((1, tc // CHUNK, nv, CHUNK), lambda b, s: (b, s, 0, 0)),
            pl.BlockSpec((1, dh), lambda b, s: (0, 0)),
        ],
        out_specs=tok(GDN_V_WIDTH),
        out_shape=jax.ShapeDtypeStruct((B, S, GDN_V_WIDTH), BF16),
        scratch_shapes=[pltpu.VMEM((nv, dh, dh), F32)],
        compiler_params=_params("arbitrary", "arbitrary"),
        name="gdn_delta",
    )(q, k, kt, v, z, bg, gt, nw.reshape(1, dh))


def _fox_in_kernel(x_ref, mod_ref, nw_ref, wt_ref, wz_ref, wft_ref, fb_ref, qw_ref, kw_ref,
                   qt_ref, kt_ref, vt_ref, z_ref, cum_ref, carry_ref, *, tm):
    s = pl.program_id(1)
    W = FOX_WIDTH
    dh = FOX_HEAD_DIM

    @pl.when(s == 0)
    def _():
        carry_ref[...] = jnp.zeros(carry_ref.shape, F32)

    h = _modulated_norm(x_ref[0], mod_ref[0, 0], nw_ref[...])
    hb = h.astype(BF16)

    grp = FOX_GROUP
    for g in range(3 * W // grp):
        blk = _dot_nt(wt_ref[g * grp:(g + 1) * grp, :], hb)
        r0 = (g * grp) % W
        if g * grp < 2 * W:
            is_q = g * grp < W
            wcol = qw_ref[...] if is_q else kw_ref[...]
            dst = qt_ref if is_q else kt_ref
            for hh in range(grp // dh):
                seg = blk[hh * dh:(hh + 1) * dh]
                inv = lax.rsqrt(jnp.mean(seg * seg, axis=0, keepdims=True) + EPS)
                dst[0, r0 + hh * dh:r0 + (hh + 1) * dh, :] = (seg * inv * wcol).astype(BF16)
        else:
            vt_ref[0, r0:r0 + grp, :] = blk.astype(BF16)
    z_ref[0] = _dot(hb, wz_ref[...]).astype(BF16)

    xf = _dot_nt(wft_ref[...], hb) + fb_ref[...]
    log_f = -_softplus(-xf)
    row = lax.broadcasted_iota(jnp.int32, (tm, tm), 0)
    col = lax.broadcasted_iota(jnp.int32, (tm, tm), 1)
    tri_up = jnp.where(row <= col, 1.0, 0.0).astype(BF16)
    cum = sum(_dot(p, tri_up) for p in _split3(log_f)) + carry_ref[...]
    cum_ref[0] = cum
    carry_ref[...] = cum[:, tm - 1:tm]


def _fox_in(x, mod, nw, w_qkv_t, w_z, w_ft, f_bias, qw, kw):
    B, S, D = x.shape
    tm = TM_PROJ
    H = FOX_HEADS
    dh = FOX_HEAD_DIM
    W = FOX_WIDTH
    kern = functools.partial(_fox_in_kernel, tm=tm)
    feat_major = pl.BlockSpec((1, W, tm), lambda b, s: (b, 0, s))
    fm_shape = jax.ShapeDtypeStruct((B, W, S), BF16)
    return pl.pallas_call(
        kern,
        grid=(B, S // tm),
        in_specs=[
            pl.BlockSpec((1, tm, D), lambda b, s: (b, s, 0)),
            pl.BlockSpec((1, 1, 3, D), lambda b, s: (b, 0, 0, 0)),
            _resident((1, D)),
            _resident(w_qkv_t.shape),
            _resident(w_z.shape),
            _resident(w_ft.shape),
            _resident((H, 1)),
            _resident((dh, 1)),
            _resident((dh, 1)),
        ],
        out_specs=[
            feat_major, feat_major, feat_major,
            pl.BlockSpec((1, tm, W), lambda b, s: (b, s, 0)),
            pl.BlockSpec((1, H, tm), lambda b, s: (b, 0, s)),
        ],
        out_shape=[
            fm_shape, fm_shape, fm_shape,
            jax.ShapeDtypeStruct((B, S, W), BF16),
            jax.ShapeDtypeStruct((B, H, S), F32),
        ],
        scratch_shapes=[pltpu.VMEM((H, 1), F32)],
        compiler_params=_params("arbitrary", "arbitrary"),
        name="fox_in",
    )(x, mod.reshape(B, 1, 3, D), nw.reshape(1, D), w_qkv_t, w_z, w_ft, f_bias.reshape(H, 1),
      qw.reshape(dh, 1), kw.reshape(dh, 1))


def _fox_attn_kernel(qt_ref, kt_ref, vt_ref, z_ref, cum_ref, o_ref,
                     qa_ref, ka_ref, va_ref, r_ref, m_ref, acc_ref, st_ref, p_ref, *, tq, seq):
    hp = pl.program_id(1)
    dh = FOX_HEAD_DIM
    pair = LANES // dh
    nq = seq // tq
    nsplit = 3
    heads = range(pair)
    kidx = lax.broadcasted_iota(jnp.int32, (tq, tq), 0)
    qidx = lax.broadcasted_iota(jnp.int32, (tq, tq), 1)
    causal = kidx <= qidx

    aug_row = lax.broadcasted_iota(jnp.int32, (BF16_ROWS, seq), 0)
    pick = jnp.where(aug_row < nsplit, 1.0, 0.0).astype(BF16)
    one_row = jnp.where(aug_row < 1, 1.0, 0.0).astype(BF16)
    for e in heads:
        qa_ref[e, 0:dh, :] = qt_ref[0, e * dh:(e + 1) * dh, :]
        qa_ref[e, dh:dh + BF16_ROWS, :] = pick
        qa_ref[e, dh + BF16_ROWS:, :] = jnp.zeros((LANES - dh - BF16_ROWS, seq), BF16)
        va_ref[e, 0:dh, :] = vt_ref[0, e * dh:(e + 1) * dh, :]
        va_ref[e, dh:, :] = one_row
        crow = cum_ref[0, pl.ds(pair * hp + e, 1), :]
        firsts = [jnp.broadcast_to(crow[:, j * tq:j * tq + 1], (1, tq)) for j in range(nq)]
        for j in range(nq):
            r_ref[e, j] = firsts[j] * (-LOG2E)
        rel = (crow - jnp.concatenate(firsts, axis=1)) * (-LOG2E)
        parts = [p.astype(F32) for p in _split3(rel)]
        btile = jnp.concatenate(parts + [jnp.zeros((SUBLANES - nsplit, seq), F32)], axis=0)
        for blk in range(seq // KT_BLOCK):
            sl = slice(blk * KT_BLOCK, (blk + 1) * KT_BLOCK)
            top = jnp.concatenate([kt_ref[0, e * dh:(e + 1) * dh, sl].astype(F32), btile[:, sl],
                                   jnp.zeros((LANES - dh - SUBLANES, KT_BLOCK), F32)], axis=0)
            ka_ref[e, sl, :] = top.T.astype(BF16)

    m_ref[...] = jnp.full(m_ref.shape, -jnp.inf, F32)
    acc_ref[...] = jnp.zeros(acc_ref.shape, F32)
    p_ref[...] = jnp.zeros(p_ref.shape, BF16)

    def scores(e, qi, kj):
        return _dot(ka_ref[e, pl.ds(pl.multiple_of(kj * tq, tq), tq), :],
                    qa_ref[e, :, pl.ds(pl.multiple_of(qi * tq, tq), tq)])

    def weighted_values(e, kj, p):
        return _dot(va_ref[e, :, pl.ds(pl.multiple_of(kj * tq, tq), tq)], p)

    def softmax_update(e, qi, kj, read_scores):
        off = r_ref[e, kj]
        m_rel = m_ref[e, qi] - off
        m_new = jnp.maximum(m_rel, jnp.max(read_scores(), axis=0, keepdims=True))
        m_ref[e, qi] = m_new + off
        return jnp.exp2(m_rel - m_new), jnp.exp2(read_scores() - m_new).astype(BF16)

    def mask_diagonal(st, masked):
        return jnp.where(causal, st, -jnp.inf) if masked else st

    def simple_step(pr, masked):
        qi, kj = pr
        for e in heads:
            st = mask_diagonal(scores(e, qi, kj), masked)
            alpha, p = softmax_update(e, qi, kj, lambda st=st: st)
            acc_ref[e, qi] = alpha * acc_ref[e, qi] + weighted_values(e, kj, p)

    def sweep(first, advance, npairs, masked, unroll):
        n_main = (npairs // unroll) * unroll
        cur = first
        if n_main:
            second = advance(*first)
            for e in heads:
                st_ref[e, 0] = scores(e, *first)
                st_ref[e, 1] = scores(e, *second)

            def body(_, carry):
                prev, cur, nxt, a_prev = carry
                for i in range(unroll):
                    nxt2 = advance(*nxt)
                    new_a = []
                    for e in heads:
                        pv_prev = weighted_values(e, prev[1], p_ref[e, (i - 1) % P_RING])
                        st_ref[e, (i + 2) % ST_RING] = scores(e, *nxt2)
                        alpha, p = softmax_update(
                            e, cur[0], cur[1], lambda e=e, i=i: mask_diagonal(st_ref[e, i % ST_RING], masked))
                        p_ref[e, i % P_RING] = p
                        acc_ref[e, prev[0]] = a_prev[e] * acc_ref[e, prev[0]] + pv_prev
                        new_a.append(alpha)
                    prev, cur, nxt, a_prev = cur, nxt, nxt2, tuple(new_a)
                return prev, cur, nxt, a_prev

            ones = tuple(jnp.ones((1, tq), F32) for _ in heads)
            spare = (jnp.int32(nq), jnp.int32(0))
            prev, cur, _, a_prev = lax.fori_loop(0, n_main // unroll, body, (spare, first, second, ones))
            for e in heads:
                pv_prev = weighted_values(e, prev[1], p_ref[e, (unroll - 1) % P_RING])
                acc_ref[e, prev[0]] = a_prev[e] * acc_ref[e, prev[0]] + pv_prev

        def tail(_, pr):
            simple_step(pr, masked)
            return advance(*pr)

        lax.fori_loop(0, npairs - n_main, tail, cur)

    def next_off_diagonal(qi, kj):
        k2 = kj + 1
        wrap = k2 >= qi
        return jnp.where(wrap, jnp.minimum(qi + 1, nq - 1), qi), jnp.where(wrap, 0, k2)

    def next_diagonal(qi, kj):
        nxt = jnp.minimum(qi + 1, nq - 1)
        return nxt, nxt

    zero = jnp.int32(0)
    sweep((jnp.int32(min(1, nq - 1)), zero), next_off_diagonal, nq * (nq - 1) // 2, False, ATTN_UNROLL)
    sweep((zero, zero), next_diagonal, nq, True, ATTN_UNROLL_DIAG)

    def finish(qi, _):
        accs = [acc_ref[e, qi] for e in heads]
        o = jnp.concatenate([a[0:dh] * (1.0 / a[dh:dh + 1]) for a in accs], axis=0).T
        rows = pl.ds(pl.multiple_of(qi * tq, tq), tq)
        zf = z_ref[0, rows, :].astype(F32)
        o_ref[0, rows, :] = (o * _silu(zf)).astype(BF16)
        return 0

    lax.fori_loop(0, nq, finish, 0)


def _fox_attn(qt, kt, vt, z, cum):
    B, W, S = qt.shape
    H = FOX_HEADS
    dh = FOX_HEAD_DIM
    pair = LANES // dh
    tq = TQ_ATTN
    nq = S // tq
    kern = functools.partial(_fox_attn_kernel, tq=tq, seq=S)
    fm = pl.BlockSpec((1, LANES, S), lambda b, h: (b, h, 0))
    return pl.pallas_call(
        kern,
        grid=(B, H // pair),
        in_specs=[
            fm, fm, fm,
            pl.BlockSpec((1, S, LANES), lambda b, h: (b, 0, h)),
            pl.BlockSpec((1, H, S), lambda b, h: (b, 0, 0)),
        ],
        out_specs=pl.BlockSpec((1, S, LANES), lambda b, h: (b, 0, h)),
        out_shape=jax.ShapeDtypeStruct((B, S, W), BF16),
        scratch_shapes=[
            pltpu.VMEM((pair, LANES, S), BF16),
            pltpu.VMEM((pair, S, LANES), BF16),
            pltpu.VMEM((pair, dh + BF16_ROWS, S), BF16),
            pltpu.VMEM((pair, nq, 1, tq), F32),
            pltpu.VMEM((pair, nq, 1, tq), F32),
            pltpu.VMEM((pair, nq + 1, dh + BF16_ROWS, tq), F32),
            pltpu.VMEM((pair, ST_RING, tq, tq), F32),
            pltpu.VMEM((pair, P_RING, tq, tq), BF16),
        ],
        compiler_params=_params("arbitrary", "arbitrary"),
        name="fox_attn",
    )(qt, kt, vt, z, cum)


def _out_kernel(o_ref, x_ref, mod_ref, w_ref, fnw_ref, y_ref, *, final):
    y = _dot(o_ref[0], w_ref[...])
    xn = x_ref[0] + mod_ref[0, 0][2:3, :] * y
    if final:
        ms = jnp.mean(xn * xn, axis=-1, keepdims=True)
        xn = xn * lax.rsqrt(ms + EPS) * fnw_ref[...]
    y_ref[0] = xn


def _out_proj(o, x, mod, w_out, fnw, final):
    B, S, D = x.shape
    width = o.shape[-1]
    tm = TM_PROJ
    kern = functools.partial(_out_kernel, final=final)
    return pl.pallas_call(
        kern,
        grid=(B, S // tm),
        in_specs=[
            pl.BlockSpec((1, tm, width), lambda b, s: (b, s, 0)),
            pl.BlockSpec((1, tm, D), lambda b, s: (b, s, 0)),
            pl.BlockSpec((1, 1, 3, D), lambda b, s: (b, 0, 0, 0)),
            _resident(w_out.shape),
            _resident((1, D)),
        ],
        out_specs=pl.BlockSpec((1, tm, D), lambda b, s: (b, s, 0)),
        out_shape=jax.ShapeDtypeStruct((B, S, D), F32),
        compiler_params=_params("arbitrary", "arbitrary"),
        name="out_proj",
    )(o, x, mod.reshape(B, 1, 3, D), w_out, fnw.reshape(1, D))


def kernel(x, c, norm_w, ada_w, ada_b, a_w_in, a_conv_w, a_A_log, a_dt_bias, a_norm_w, a_w_out,
           b_w_in, b_f_bias, b_qn_w, b_kn_w, b_w_out, final_norm_w):
    B, S, D = x.shape
    assert D == D_MODEL and S % max(TM_PROJ, TM_GDN_IN, TC_DELTA, TQ_ATTN) == 0
    mods = _ada_mod(c, ada_w, ada_b)
    nh = GDN_V_HEADS
    for i in range(DEPTH):
        j = i // 2
        final = i == DEPTH - 1
        if i % 2 == 0:
            w_in = a_w_in[j]
            w_main = w_in[:, :GDN_CONV_CH + GDN_V_WIDTH].astype(BF16)
            w_b = w_in[:, GDN_CONV_CH + GDN_V_WIDTH:GDN_CONV_CH + GDN_V_WIDTH + nh]
            w_a = w_in[:, GDN_CONV_CH + GDN_V_WIDTH + nh:]
            pad = jnp.zeros((D, LANES - nh), F32)
            w_ba = jnp.concatenate([w_b, pad, w_a, pad], axis=1).astype(BF16)
            w_at = w_a.T.astype(BF16)
            q, k, kt, v, z, bg, gt = _gdn_in(x, mods[i], norm_w[i], w_main, w_ba, w_at, a_conv_w[j],
                                              a_A_log[j], a_dt_bias[j])
            o = _gdn_delta(q, k, kt, v, z, bg, gt, a_norm_w[j])
            w_out = a_w_out[j].astype(BF16)
        else:
            w_in = b_w_in[j]
            w_qkv_t = w_in[:, :3 * FOX_WIDTH].T.astype(BF16)
            w_z = w_in[:, 3 * FOX_WIDTH:4 * FOX_WIDTH].astype(BF16)
            w_ft = w_in[:, 4 * FOX_WIDTH:].T.astype(BF16)
            qw = b_qn_w[j] * (FOX_HEAD_DIM ** -0.5 * LOG2E)
            qt, kt, vt, z, cum = _fox_in(x, mods[i], norm_w[i], w_qkv_t, w_z, w_ft, b_f_bias[j], qw, b_kn_w[j])
            o = _fox_attn(qt, kt, vt, z, cum)
            w_out = b_w_out[j].astype(BF16)
        x = _out_proj(o, x, mods[i], w_out, final_norm_w, final)
    return x
```

```python
import functools

import jax
import jax.numpy as jnp
from jax import lax
from jax.experimental import pallas as pl
from jax.experimental.pallas import tpu as pltpu

F32 = jnp.float32
BF16 = jnp.bfloat16

D_MODEL = 1024
DEPTH = 4
EPS = 1e-6
CHUNK = 64

GDN_QK_HEADS = 8
GDN_V_HEADS = 16
GDN_HEAD_DIM = 128
GDN_QK_WIDTH = GDN_QK_HEADS * GDN_HEAD_DIM
GDN_V_WIDTH = GDN_V_HEADS * GDN_HEAD_DIM
GDN_CONV_CH = 2 * GDN_QK_WIDTH + GDN_V_WIDTH
CONV_WIDTH = 4

FOX_HEADS = 16
FOX_HEAD_DIM = 64
FOX_WIDTH = FOX_HEADS * FOX_HEAD_DIM

LANES = 128
SUBLANES = 8
BF16_ROWS = 16
LOG2E = 1.4426950408889634
VMEM_LIMIT = 48 * 1024 * 1024

TM_PROJ = 512
TM_GDN_IN = 512
TC_DELTA = 512
DELTA_CHUNKS_PER_TRIP = 2
TQ_ATTN = 256
CONV_GROUP = 512
FOX_GROUP = 512
KT_BLOCK = 512
ATTN_UNROLL = 60
ATTN_UNROLL_DIAG = 16
FINISH_UNROLL = 4
ST_RING = 4
P_RING = 2


def _sigmoid(x):
    return 1.0 / (1.0 + jnp.exp(-x))


def _silu(x):
    return x * _sigmoid(x)


def _softplus(x):
    return jnp.maximum(x, 0.0) + jnp.log(1.0 + jnp.exp(-jnp.abs(x)))


def _split3(a):
    hi = a.astype(BF16)
    r = a - hi.astype(F32)
    mid = r.astype(BF16)
    lo = (r - mid.astype(F32)).astype(BF16)
    return hi, mid, lo


def _dot(a, b):
    return jnp.dot(a, b, preferred_element_type=F32)


def _dot_nt(a, b):
    return lax.dot_general(a, b, (((1,), (1,)), ((), ())), preferred_element_type=F32)


def _dot_tn(a, b):
    return lax.dot_general(a, b, (((0,), (0,)), ((), ())), preferred_element_type=F32)


def _modulated_norm(x, mod, nw):
    ms = jnp.mean(x * x, axis=-1, keepdims=True)
    y = x * lax.rsqrt(ms + EPS) * nw
    return y * (1.0 + mod[1:2, :]) + mod[0:1, :]


def _params(*sem):
    return pltpu.CompilerParams(dimension_semantics=sem, vmem_limit_bytes=VMEM_LIMIT)


def _resident(shape):
    nd = len(shape)
    return pl.BlockSpec(shape, lambda *_: (0,) * nd, pipeline_mode=pl.Buffered(1))


def _ada_kernel(c_ref, w_ref, b_ref, o_ref):
    cond = _silu(c_ref[...])
    o_ref[0] = _dot(cond, w_ref[0]) + b_ref[0]


def _ada_mod(c, ada_w, ada_b):
    B, D = c.shape
    depth = ada_w.shape[0]
    out = pl.pallas_call(
        _ada_kernel,
        grid=(depth,),
        in_specs=[
            pl.BlockSpec((B, D), lambda i: (0, 0)),
            pl.BlockSpec((1, D, 3 * D), lambda i: (i, 0, 0)),
            pl.BlockSpec((1, 1, 3 * D), lambda i: (i, 0, 0)),
        ],
        out_specs=pl.BlockSpec((1, B, 3 * D), lambda i: (i, 0, 0)),
        out_shape=jax.ShapeDtypeStruct((depth, B, 3 * D), F32),
        compiler_params=_params("arbitrary"),
        name="ada_mod",
    )(c, ada_w, ada_b.reshape(depth, 1, 3 * D))
    return out.reshape(depth, B, 3, D)


def _gdn_in_kernel(x_ref, mod_ref, nw_ref, w_ref, wba_ref, wbat_ref, cw_ref,
                   alog_ref, dtb_ref, alogc_ref, dtbc_ref,
                   q_ref, k_ref, kt_ref, v_ref, z_ref, bg_ref, gt_ref, cbuf_ref, *, tm):
    s = pl.program_id(1)
    tail = SUBLANES
    nchunk = tm // CHUNK

    @pl.when(s == 0)
    def _():
        cbuf_ref[0:tail, :] = jnp.zeros((tail, GDN_CONV_CH), F32)

    h = _modulated_norm(x_ref[0], mod_ref[0, 0], nw_ref[...])
    hb = h.astype(BF16)

    qscale = GDN_HEAD_DIM ** -0.5
    for c in range(GDN_CONV_CH // CONV_GROUP):
        lo = c * CONV_GROUP
        pre = _dot(hb, w_ref[:, lo:lo + CONV_GROUP])
        cbuf_ref[tail:tail + tm, lo:lo + CONV_GROUP] = pre
        acc = cw_ref[CONV_WIDTH - 1:CONV_WIDTH, lo:lo + CONV_GROUP] * pre
        for j in range(CONV_WIDTH - 1):
            off = tail - (CONV_WIDTH - 1) + j
            acc = acc + cw_ref[j:j + 1, lo:lo + CONV_GROUP] * cbuf_ref[off:off + tm, lo:lo + CONV_GROUP]
        y = _silu(acc)
        if lo < 2 * GDN_QK_WIDTH:
            for hh in range(CONV_GROUP // GDN_HEAD_DIM):
                seg = y[:, hh * GDN_HEAD_DIM:(hh + 1) * GDN_HEAD_DIM]
                inv = lax.rsqrt(jnp.sum(seg * seg, axis=-1, keepdims=True) + EPS)
                col = lo + hh * GDN_HEAD_DIM
                if col < GDN_QK_WIDTH:
                    q_ref[0, :, col:col + GDN_HEAD_DIM] = (seg * inv * qscale).astype(BF16)
                else:
                    col -= GDN_QK_WIDTH
                    kn = seg * inv
                    k_ref[0, :, col:col + GDN_HEAD_DIM] = kn.astype(BF16)
                    knt = kn.T.astype(BF16)
                    for cc in range(nchunk):
                        kt_ref[0, cc, col:col + GDN_HEAD_DIM, :] = knt[:, cc * CHUNK:(cc + 1) * CHUNK]
        else:
            col = lo - 2 * GDN_QK_WIDTH
            v_ref[0, :, col:col + CONV_GROUP] = y.astype(BF16)
    cbuf_ref[0:tail, :] = cbuf_ref[tm:tm + tail, :]

    for c in range(GDN_V_WIDTH // CONV_GROUP):
        lo = c * CONV_GROUP
        z = _dot(hb, w_ref[:, GDN_CONV_CH + lo:GDN_CONV_CH + lo + CONV_GROUP])
        z_ref[0, :, lo:lo + CONV_GROUP] = z.astype(BF16)

    nh = GDN_V_HEADS
    ba = _dot(hb, wba_ref[...])
    beta = _sigmoid(ba[:, 0:nh])
    g = -jnp.exp(alog_ref[...]) * _softplus(ba[:, LANES:LANES + nh] + dtb_ref[...])
    row = lax.broadcasted_iota(jnp.int32, (tm, tm), 0)
    col = lax.broadcasted_iota(jnp.int32, (tm, tm), 1)
    same = (row // CHUNK) == (col // CHUNK)
    tri_lo = jnp.where(same & (col <= row), 1.0, 0.0).astype(BF16)
    tri_up = jnp.where(same & (row <= col), 1.0, 0.0).astype(BF16)
    gc = sum(_dot(tri_lo, p) for p in _split3(g))
    bg_ref[0, :, 0:nh] = beta
    bg_ref[0, :, nh:2 * nh] = gc
    bat = _dot_nt(wbat_ref[...], hb)
    g_t = -jnp.exp(alogc_ref[...]) * _softplus(bat + dtbc_ref[...])
    gct = sum(_dot(p, tri_up) for p in _split3(g_t))
    for cc in range(nchunk):
        gt_ref[0, cc] = gct[:, cc * CHUNK:(cc + 1) * CHUNK]


def _gdn_in(x, mod, nw, w_main, w_ba, w_at, conv_w, a_log, dt_bias):
    B, S, D = x.shape
    tm = TM_GDN_IN
    nh = GDN_V_HEADS
    kern = functools.partial(_gdn_in_kernel, tm=tm)
    tok = lambda width: pl.BlockSpec((1, tm, width), lambda b, s: (b, s, 0))
    return pl.pallas_call(
        kern,
        grid=(B, S // tm),
        in_specs=[
            tok(D),
            pl.BlockSpec((1, 1, 3, D), lambda b, s: (b, 0, 0, 0)),
            _resident((1, D)),
            _resident(w_main.shape),
            _resident(w_ba.shape),
            _resident(w_at.shape),
            _resident(conv_w.shape),
            _resident((1, nh)), _resident((1, nh)), _resident((nh, 1)), _resident((nh, 1)),
        ],
        out_specs=[
            tok(GDN_QK_WIDTH), tok(GDN_QK_WIDTH),
            pl.BlockSpec((1, tm // CHUNK, GDN_QK_WIDTH, CHUNK), lambda b, s: (b, s, 0, 0)),
            tok(GDN_V_WIDTH), tok(GDN_V_WIDTH),
            tok(2 * nh),
            pl.BlockSpec((1, tm // CHUNK, nh, CHUNK), lambda b, s: (b, s, 0, 0)),
        ],
        out_shape=[
            jax.ShapeDtypeStruct((B, S, GDN_QK_WIDTH), BF16),
            jax.ShapeDtypeStruct((B, S, GDN_QK_WIDTH), BF16),
            jax.ShapeDtypeStruct((B, S // CHUNK, GDN_QK_WIDTH, CHUNK), BF16),
            jax.ShapeDtypeStruct((B, S, GDN_V_WIDTH), BF16),
            jax.ShapeDtypeStruct((B, S, GDN_V_WIDTH), BF16),
            jax.ShapeDtypeStruct((B, S, 2 * nh), F32),
            jax.ShapeDtypeStruct((B, S // CHUNK, nh, CHUNK), F32),
        ],
        scratch_shapes=[pltpu.VMEM((tm + SUBLANES, GDN_CONV_CH), F32)],
        compiler_params=_params("arbitrary", "arbitrary"),
        name="gdn_in",
    )(x, mod.reshape(B, 1, 3, D), nw.reshape(1, D), w_main, w_ba, w_at, conv_w,
      a_log.reshape(1, nh), dt_bias.reshape(1, nh), a_log.reshape(nh, 1), dt_bias.reshape(nh, 1))


def _delta_kernel(q_ref, k_ref, kt_ref, v_ref, z_ref, bg_ref, gt_ref, nw_ref, o_ref, s_ref, *, tc):
    s = pl.program_id(1)
    C = CHUNK
    dh = GDN_HEAD_DIM
    nqk = GDN_QK_HEADS
    nv = GDN_V_HEADS
    rep = nv // nqk
    heads = range(nv)

    @pl.when(s == 0)
    def _():
        s_ref[...] = jnp.zeros(s_ref.shape, F32)

    row = lax.broadcasted_iota(jnp.int32, (C, C), 0)
    col = lax.broadcasted_iota(jnp.int32, (C, C), 1)
    lower = row >= col
    strict = row > col
    eye = jnp.where(row == col, 1.0, 0.0).astype(F32)
    nw = nw_ref[...]

    nper = DELTA_CHUNKS_PER_TRIP
    units = [(t, h) for t in range(nper) for h in heads]

    def trip(i, _):
        cs = [i * nper + t for t in range(nper)]
        rows = [pl.ds(pl.multiple_of(c * C, C), C) for c in cs]
        bg = [bg_ref[0, rows[t], :] for t in range(nper)]
        gt = [gt_ref[0, cs[t]] for t in range(nper)]
        groups = [(t, j) for t in range(nper) for j in range(nqk)]
        qb = {(t, j): q_ref[0, rows[t], j * dh:(j + 1) * dh] for t, j in groups}
        kb = {(t, j): k_ref[0, rows[t], j * dh:(j + 1) * dh] for t, j in groups}
        ktb = {(t, j): kt_ref[0, cs[t], j * dh:(j + 1) * dh, :] for t, j in groups}
        kq = {g: _dot(jnp.concatenate([kb[g], qb[g]], axis=0), ktb[g]) for g in groups}
        grp = {(t, h): (t, h // rep) for t, h in units}

        beta = {(t, h): bg[t][:, h:h + 1] for t, h in units}
        gc = {(t, h): bg[t][:, nv + h:nv + h + 1] for t, h in units}
        gr = {(t, h): gt[t][h:h + 1, :] for t, h in units}
        g_last = {u: gr[u][:, C - 1:C] for u in units}
        decay = {u: jnp.exp(jnp.where(lower, gc[u] - gr[u], -jnp.inf)) for u in units}
        egc = {u: jnp.exp(gc[u]) for u in units}

        X = {u: -(jnp.where(strict, kq[grp[u]][:C] * decay[u], 0.0) * beta[u]) for u in units}
        negL = X
        P = {u: eye + X[u] for u in units}
        Xb = {u: X[u].astype(BF16) for u in units}
        X = {u: _dot(Xb[u], Xb[u]) for u in units}
        p = 4
        while p < C // 2:
            Xb = {u: X[u].astype(BF16) for u in units}
            PX = {u: _dot(jnp.concatenate([P[u], X[u]], axis=0).astype(BF16), Xb[u]) for u in units}
            P = {u: P[u] + PX[u][:C] for u in units}
            X = {u: PX[u][C:] for u in units}
            p *= 2
        P = {u: P[u] + _dot(P[u].astype(BF16), X[u].astype(BF16)) for u in units}
        Tb, rest = {}, {}
        for u in units:
            Tb[u] = P[u].astype(BF16)
            nl_hi = negL[u].astype(BF16)
            nl_lo = (negL[u] - nl_hi.astype(F32)).astype(BF16)
            hl = _dot(jnp.concatenate([nl_hi, nl_lo], axis=0), Tb[u])
            rest[u] = (eye - Tb[u].astype(F32)) + (hl[:C] + hl[C:])
        P = {u: Tb[u].astype(F32) + _dot(Tb[u], rest[u].astype(BF16)) for u in units}

        uw = {}
        for t, h in units:
            u = (t, h)
            vf = v_ref[0, rows[t], h * dh:(h + 1) * dh].astype(F32)
            kf = kb[grp[u]].astype(F32)
            rhs = jnp.concatenate([vf * beta[u], kf * (beta[u] * egc[u])], axis=1).astype(BF16)
            uw[u] = _dot(P[u].astype(BF16), rhs)
        wq = {u: jnp.concatenate([uw[u][:, dh:], qb[grp[u]].astype(F32) * egc[u]], axis=0).astype(BF16)
              for u in units}
        ak_lhs = {}
        for u in units:
            attn = (kq[grp[u]][C:] * decay[u]).astype(BF16)
            kdt = (ktb[grp[u]].astype(F32) * jnp.exp(g_last[u] - gr[u])).astype(BF16)
            ak_lhs[u] = jnp.concatenate([attn, kdt], axis=0)
        for t in range(nper):
            S = [s_ref[h] for h in heads]
            r = [_dot(wq[(t, h)], S[h].astype(BF16)) for h in heads]
            vnb = [(uw[(t, h)][:, :dh] - r[h][:C]).astype(BF16) for h in heads]
            ak = [_dot(ak_lhs[(t, h)], vnb[h]) for h in heads]
            for h in heads:
                s_ref[h] = S[h] * jnp.exp(g_last[(t, h)]) + ak[h][C:]
            for h in heads:
                o = r[h][C:] + ak[h][:C]
                var = jnp.mean(o * o, axis=-1, keepdims=True)
                zf = z_ref[0, rows[t], h * dh:(h + 1) * dh].astype(F32)
                o_ref[0, rows[t], h * dh:(h + 1) * dh] = (o * lax.rsqrt(var + EPS) * nw * _silu(zf)).astype(BF16)
        return 0

    lax.fori_loop(0, tc // (C * nper), trip, 0)


def _gdn_delta(q, k, kt, v, z, bg, gt, nw):
    B, S, _ = q.shape
    tc = TC_DELTA
    nv = GDN_V_HEADS
    dh = GDN_HEAD_DIM
    kern = functools.partial(_delta_kernel, tc=tc)
    tok = lambda width: pl.BlockSpec((1, tc, width), lambda b, s: (b, s, 0))
    return pl.pallas_call(
        kern,
        grid=(B, S // tc),
        in_specs=[
            tok(GDN_QK_WIDTH), tok(GDN_QK_WIDTH),
            pl.BlockSpec((1, tc // CHUNK, GDN_QK_WIDTH, CHUNK), lambda b, s: (b, s, 0, 0)),
            tok(GDN_V_WIDTH), tok(GDN_V_WIDTH),
            tok(2 * nv),
            pl.BlockSpec((1, tc // CHUNK, nv, CHUNK), lambda b, s: (b, s, 0, 0)),
            pl.BlockSpec((1, dh), lambda b, s: (0, 0)),
        ],
        out_specs=tok(GDN_V_WIDTH),
        out_shape=jax.ShapeDtypeStruct((B, S, GDN_V_WIDTH), BF16),
        scratch_shapes=[pltpu.VMEM((nv, dh, dh), F32)],
        compiler_params=_params("arbitrary", "arbitrary"),
        name="gdn_delta",
    )(q, k, kt, v, z, bg, gt, nw.reshape(1, dh))


def _fox_in_kernel(x_ref, mod_ref, nw_ref, wt_ref, wz_ref, wft_ref, fb_ref, qw_ref, kw_ref,
                   qt_ref, kt_ref, vt_ref, z_ref, cum_ref, carry_ref, *, tm):
    s = pl.program_id(1)
    W = FOX_WIDTH
    dh = FOX_HEAD_DIM

    @pl.when(s == 0)
    def _():
        carry_ref[...] = jnp.zeros(carry_ref.shape, F32)

    h = _modulated_norm(x_ref[0], mod_ref[0, 0], nw_ref[...])
    hb = h.astype(BF16)

    grp = FOX_GROUP
    for g in range(3 * W // grp):
        blk = _dot_nt(wt_ref[g * grp:(g + 1) * grp, :], hb)
        r0 = (g * grp) % W
        if g * grp < 2 * W:
            is_q = g * grp < W
            wcol = qw_ref[...] if is_q else kw_ref[...]
            dst = qt_ref if is_q else kt_ref
            for hh in range(grp // dh):
                seg = blk[hh * dh:(hh + 1) * dh]
                inv = lax.rsqrt(jnp.mean(seg * seg, axis=0, keepdims=True) + EPS)
                dst[0, r0 + hh * dh:r0 + (hh + 1) * dh, :] = (seg * inv * wcol).astype(BF16)
        else:
            vt_ref[0, r0:r0 + grp, :] = blk.astype(BF16)
    z_ref[0] = _dot(hb, wz_ref[...]).astype(BF16)

    xf = _dot_nt(wft_ref[...], hb) + fb_ref[...]
    log_f = -_softplus(-xf)
    row = lax.broadcasted_iota(jnp.int32, (tm, tm), 0)
    col = lax.broadcasted_iota(jnp.int32, (tm, tm), 1)
    tri_up = jnp.where(row <= col, 1.0, 0.0).astype(BF16)
    cum = sum(_dot(p, tri_up) for p in _split3(log_f)) + carry_ref[...]
    cum_ref[0] = cum
    carry_ref[...] = cum[:, tm - 1:tm]


def _fox_in(x, mod, nw, w_qkv_t, w_z, w_ft, f_bias, qw, kw):
    B, S, D = x.shape
    tm = TM_PROJ
    H = FOX_HEADS
    dh = FOX_HEAD_DIM
    W = FOX_WIDTH
    kern = functools.partial(_fox_in_kernel, tm=tm)
    feat_major = pl.BlockSpec((1, W, tm), lambda b, s: (b, 0, s))
    fm_shape = jax.ShapeDtypeStruct((B, W, S), BF16)
    return pl.pallas_call(
        kern,
        grid=(B, S // tm),
        in_specs=[
            pl.BlockSpec((1, tm, D), lambda b, s: (b, s, 0)),
            pl.BlockSpec((1, 1, 3, D), lambda b, s: (b, 0, 0, 0)),
            _resident((1, D)),
            _resident(w_qkv_t.shape),
            _resident(w_z.shape),
            _resident(w_ft.shape),
            _resident((H, 1)),
            _resident((dh, 1)),
            _resident((dh, 1)),
        ],
        out_specs=[
            feat_major, feat_major, feat_major,
            pl.BlockSpec((1, tm, W), lambda b, s: (b, s, 0)),
            pl.BlockSpec((1, H, tm), lambda b, s: (b, 0, s)),
        ],
        out_shape=[
            fm_shape, fm_shape, fm_shape,
            jax.ShapeDtypeStruct((B, S, W), BF16),
            jax.ShapeDtypeStruct((B, H, S), F32),
        ],
        scratch_shapes=[pltpu.VMEM((H, 1), F32)],
        compiler_params=_params("arbitrary", "arbitrary"),
        name="fox_in",
    )(x, mod.reshape(B, 1, 3, D), nw.reshape(1, D), w_qkv_t, w_z, w_ft, f_bias.reshape(H, 1),
      qw.reshape(dh, 1), kw.reshape(dh, 1))


def _fox_attn_kernel(qt_ref, kt_ref, vt_ref, z_ref, cum_ref, o_ref,
                     qa_ref, ka_ref, va_ref, r_ref, m_ref, acc_ref, st_ref, p_ref, *, tq, seq):
    hp = pl.program_id(1)
    dh = FOX_HEAD_DIM
    pair = LANES // dh
    nq = seq // tq
    nsplit = 3
    heads = range(pair)
    kidx = lax.broadcasted_iota(jnp.int32, (tq, tq), 0)
    qidx = lax.broadcasted_iota(jnp.int32, (tq, tq), 1)
    causal = kidx <= qidx

    aug_row = lax.broadcasted_iota(jnp.int32, (BF16_ROWS, seq), 0)
    pick = jnp.where(aug_row < nsplit, 1.0, 0.0).astype(BF16)
    one_row = jnp.where(aug_row < 1, 1.0, 0.0).astype(BF16)
    for e in heads:
        qa_ref[e, 0:dh, :] = qt_ref[0, e * dh:(e + 1) * dh, :]
        qa_ref[e, dh:dh + BF16_ROWS, :] = pick
        qa_ref[e, dh + BF16_ROWS:, :] = jnp.zeros((LANES - dh - BF16_ROWS, seq), BF16)
        va_ref[e, 0:dh, :] = vt_ref[0, e * dh:(e + 1) * dh, :]
        va_ref[e, dh:, :] = one_row
        crow = cum_ref[0, pl.ds(pair * hp + e, 1), :]
        firsts = [jnp.broadcast_to(crow[:, j * tq:j * tq + 1], (1, tq)) for j in range(nq)]
        for j in range(nq):
            r_ref[e, j] = firsts[j] * (-LOG2E)
        rel = (crow - jnp.concatenate(firsts, axis=1)) * (-LOG2E)
        parts = [p.astype(F32) for p in _split3(rel)]
        btile = jnp.concatenate(parts + [jnp.zeros((SUBLANES - nsplit, seq), F32)], axis=0)
        for blk in range(seq // KT_BLOCK):
            sl = slice(blk * KT_BLOCK, (blk + 1) * KT_BLOCK)
            top = jnp.concatenate([kt_ref[0, e * dh:(e + 1) * dh, sl].astype(F32), btile[:, sl],
                                   jnp.zeros((LANES - dh - SUBLANES, KT_BLOCK), F32)], axis=0)
            ka_ref[e, sl, :] = top.T.astype(BF16)

    m_ref[...] = jnp.full(m_ref.shape, -jnp.inf, F32)
    acc_ref[...] = jnp.zeros(acc_ref.shape, F32)
    p_ref[...] = jnp.zeros(p_ref.shape, BF16)

    def scores(e, qi, kj):
        return _dot(ka_ref[e, pl.ds(pl.multiple_of(kj * tq, tq), tq), :],
                    qa_ref[e, :, pl.ds(pl.multiple_of(qi * tq, tq), tq)])

    def weighted_values(e, kj, p):
        return _dot(va_ref[e, :, pl.ds(pl.multiple_of(kj * tq, tq), tq)], p)

    def softmax_update(e, qi, kj, read_scores):
        off = r_ref[e, kj]
        m_rel = m_ref[e, qi] - off
        m_new = jnp.maximum(m_rel, jnp.max(read_scores(), axis=0, keepdims=True))
        m_ref[e, qi] = m_new + off
        return jnp.exp2(m_rel - m_new), jnp.exp2(read_scores() - m_new).astype(BF16)

    def mask_diagonal(st, masked):
        return jnp.where(causal, st, -jnp.inf) if masked else st

    def simple_step(pr, masked):
        qi, kj = pr
        for e in heads:
            st = mask_diagonal(scores(e, qi, kj), masked)
            alpha, p = softmax_update(e, qi, kj, lambda st=st: st)
            acc_ref[e, qi] = alpha * acc_ref[e, qi] + weighted_values(e, kj, p)

    def sweep(first, advance, npairs, masked, unroll):
        n_main = (npairs // unroll) * unroll
        cur = first
        if n_main:
            second = advance(*first)
            for e in heads:
                st_ref[e, 0] = scores(e, *first)
                st_ref[e, 1] = scores(e, *second)

            def body(_, carry):
                prev, cur, nxt, a_prev = carry
                for i in range(unroll):
                    nxt2 = advance(*nxt)
                    new_a = []
                    for e in heads:
                        pv_prev = weighted_values(e, prev[1], p_ref[e, (i - 1) % P_RING])
                        st_ref[e, (i + 2) % ST_RING] = scores(e, *nxt2)
                        alpha, p = softmax_update(
                            e, cur[0], cur[1], lambda e=e, i=i: mask_diagonal(st_ref[e, i % ST_RING], masked))
                        p_ref[e, i % P_RING] = p
                        acc_ref[e, prev[0]] = a_prev[e] * acc_ref[e, prev[0]] + pv_prev
                        new_a.append(alpha)
                    prev, cur, nxt, a_prev = cur, nxt, nxt2, tuple(new_a)
                return prev, cur, nxt, a_prev

            ones = tuple(jnp.ones((1, tq), F32) for _ in heads)
            spare = (jnp.int32(nq), jnp.int32(0))
            prev, cur, _, a_prev = lax.fori_loop(0, n_main // unroll, body, (spare, first, second, ones))
            for e in heads:
                pv_prev = weighted_values(e, prev[1], p_ref[e, (unroll - 1) % P_RING])
                acc_ref[e, prev[0]] = a_prev[e] * acc_ref[e, prev[0]] + pv_prev

        def tail(_, pr):
            simple_step(pr, masked)
            return advance(*pr)

        lax.fori_loop(0, npairs - n_main, tail, cur)

    def next_off_diagonal(qi, kj):
        k2 = kj + 1
        wrap = k2 >= qi
        return jnp.where(wrap, jnp.minimum(qi + 1, nq - 1), qi), jnp.where(wrap, 0, k2)

    def next_diagonal(qi, kj):
        nxt = jnp.minimum(qi + 1, nq - 1)
        return nxt, nxt

    zero = jnp.int32(0)
    sweep((jnp.int32(min(1, nq - 1)), zero), next_off_diagonal, nq * (nq - 1) // 2, False, ATTN_UNROLL)
    sweep((zero, zero), next_diagonal, nq, True, ATTN_UNROLL_DIAG)

    per_trip = FINISH_UNROLL if nq % FINISH_UNROLL == 0 else 1

    def finish(i, _):
        for t in range(per_trip):
            qi = i * per_trip + t
            accs = [acc_ref[e, qi] for e in heads]
            o = jnp.concatenate([a[0:dh] * (1.0 / a[dh:dh + 1]) for a in accs], axis=0).T
            rows = pl.ds(pl.multiple_of(qi * tq, tq), tq)
            zf = z_ref[0, rows, :].astype(F32)
            o_ref[0, rows, :] = (o * _silu(zf)).astype(BF16)
        return 0

    lax.fori_loop(0, nq // per_trip, finish, 0)


def _fox_attn(qt, kt, vt, z, cum):
    B, W, S = qt.shape
    H = FOX_HEADS
    dh = FOX_HEAD_DIM
    pair = LANES // dh
    tq = TQ_ATTN
    nq = S // tq
    kern = functools.partial(_fox_attn_kernel, tq=tq, seq=S)
    fm = pl.BlockSpec((1, LANES, S), lambda b, h: (b, h, 0))
    return pl.pallas_call(
        kern,
        grid=(B, H // pair),
        in_specs=[
            fm, fm, fm,
            pl.BlockSpec((1, S, LANES), lambda b, h: (b, 0, h)),
            pl.BlockSpec((1, H, S), lambda b, h: (b, 0, 0)),
        ],
        out_specs=pl.BlockSpec((1, S, LANES), lambda b, h: (b, 0, h)),
        out_shape=jax.ShapeDtypeStruct((B, S, W), BF16),
        scratch_shapes=[
            pltpu.VMEM((pair, LANES, S), BF16),
            pltpu.VMEM((pair, S, LANES), BF16),
            pltpu.VMEM((pair, dh + BF16_ROWS, S), BF16),
            pltpu.VMEM((pair, nq, 1, tq), F32),
            pltpu.VMEM((pair, nq, 1, tq), F32),
            pltpu.VMEM((pair, nq + 1, dh + BF16_ROWS, tq), F32),
            pltpu.VMEM((pair, ST_RING, tq, tq), F32),
            pltpu.VMEM((pair, P_RING, tq, tq), BF16),
        ],
        compiler_params=_params("arbitrary", "arbitrary"),
        name="fox_attn",
    )(qt, kt, vt, z, cum)


def _out_kernel(o_ref, x_ref, mod_ref, w_ref, fnw_ref, y_ref, *, final):
    y = _dot(o_ref[0], w_ref[...])
    xn = x_ref[0] + mod_ref[0, 0][2:3, :] * y
    if final:
        ms = jnp.mean(xn * xn, axis=-1, keepdims=True)
        xn = xn * lax.rsqrt(ms + EPS) * fnw_ref[...]
    y_ref[0] = xn


def _out_proj(o, x, mod, w_out, fnw, final):
    B, S, D = x.shape
    width = o.shape[-1]
    tm = TM_PROJ
    kern = functools.partial(_out_kernel, final=final)
    return pl.pallas_call(
        kern,
        grid=(B, S // tm),
        in_specs=[
            pl.BlockSpec((1, tm, width), lambda b, s: (b, s, 0)),
            pl.BlockSpec((1, tm, D), lambda b, s: (b, s, 0)),
            pl.BlockSpec((1, 1, 3, D), lambda b, s: (b, 0, 0, 0)),
            _resident(w_out.shape),
            _resident((1, D)),
        ],
        out_specs=pl.BlockSpec((1, tm, D), lambda b, s: (b, s, 0)),
        out_shape=jax.ShapeDtypeStruct((B, S, D), F32),
        compiler_params=_params("arbitrary", "arbitrary"),
        name="out_proj",
    )(o, x, mod.reshape(B, 1, 3, D), w_out, fnw.reshape(1, D))


def kernel(x, c, norm_w, ada_w, ada_b, a_w_in, a_conv_w, a_A_log, a_dt_bias, a_norm_w, a_w_out,
           b_w_in, b_f_bias, b_qn_w, b_kn_w, b_w_out, final_norm_w):
    B, S, D = x.shape
    assert D == D_MODEL and S % max(TM_PROJ, TM_GDN_IN, TC_DELTA, TQ_ATTN) == 0
    mods = _ada_mod(c, ada_w, ada_b)
    nh = GDN_V_HEADS
    for i in range(DEPTH):
        j = i // 2
        final = i == DEPTH - 1
        if i % 2 == 0:
            w_in = a_w_in[j]
            w_main = w_in[:, :GDN_CONV_CH + GDN_V_WIDTH].astype(BF16)
            w_b = w_in[:, GDN_CONV_CH + GDN_V_WIDTH:GDN_CONV_CH + GDN_V_WIDTH + nh]
            w_a = w_in[:, GDN_CONV_CH + GDN_V_WIDTH + nh:]
            pad = jnp.zeros((D, LANES - nh), F32)
            w_ba = jnp.concatenate([w_b, pad, w_a, pad], axis=1).astype(BF16)
            w_at = w_a.T.astype(BF16)
            q, k, kt, v, z, bg, gt = _gdn_in(x, mods[i], norm_w[i], w_main, w_ba, w_at, a_conv_w[j],
                                              a_A_log[j], a_dt_bias[j])
            o = _gdn_delta(q, k, kt, v, z, bg, gt, a_norm_w[j])
            w_out = a_w_out[j].astype(BF16)
        else:
            w_in = b_w_in[j]
            w_qkv_t = w_in[:, :3 * FOX_WIDTH].T.astype(BF16)
            w_z = w_in[:, 3 * FOX_WIDTH:4 * FOX_WIDTH].astype(BF16)
            w_ft = w_in[:, 4 * FOX_WIDTH:].T.astype(BF16)
            qw = b_qn_w[j] * (FOX_HEAD_DIM ** -0.5 * LOG2E)
            qt, kt, vt, z, cum = _fox_in(x, mods[i], norm_w[i], w_qkv_t, w_z, w_ft, b_f_bias[j], qw, b_kn_w[j])
            o = _fox_attn(qt, kt, vt, z, cum)
            w_out = b_w_out[j].astype(BF16)
        x = _out_proj(o, x, mods[i], w_out, final_norm_w, final)
    return x
```

```python
import functools

import jax
import jax.numpy as jnp
from jax import lax
from jax.experimental import pallas as pl
from jax.experimental.pallas import tpu as pltpu

F32 = jnp.float32
BF16 = jnp.bfloat16

D_MODEL = 1024
DEPTH = 4
EPS = 1e-6
CHUNK = 64

GDN_QK_HEADS = 8
GDN_V_HEADS = 16
GDN_HEAD_DIM = 128
GDN_QK_WIDTH = GDN_QK_HEADS * GDN_HEAD_DIM
GDN_V_WIDTH = GDN_V_HEADS * GDN_HEAD_DIM
GDN_CONV_CH = 2 * GDN_QK_WIDTH + GDN_V_WIDTH
CONV_WIDTH = 4

FOX_HEADS = 16
FOX_HEAD_DIM = 64
FOX_WIDTH = FOX_HEADS * FOX_HEAD_DIM

LANES = 128
SUBLANES = 8
BF16_ROWS = 16
LOG2E = 1.4426950408889634
VMEM_LIMIT = 48 * 1024 * 1024

TM_PROJ = 512
TM_GDN_IN = 512
TC_DELTA = 512
DELTA_CHUNKS_PER_TRIP = 2
TQ_ATTN = 256
CONV_GROUP = 512
FOX_GROUP = 512
KT_BLOCK = 512
ATTN_UNROLL = 60
ATTN_UNROLL_DIAG = 16
FINISH_UNROLL = 4
ST_RING = 4
P_RING = 2


def _sigmoid(x):
    return 1.0 / (1.0 + jnp.exp(-x))


def _silu(x):
    return x * _sigmoid(x)


def _softplus(x):
    return jnp.maximum(x, 0.0) + jnp.log(1.0 + jnp.exp(-jnp.abs(x)))


def _split3(a):
    hi = a.astype(BF16)
    r = a - hi.astype(F32)
    mid = r.astype(BF16)
    lo = (r - mid.astype(F32)).astype(BF16)
    return hi, mid, lo


def _dot(a, b):
    return jnp.dot(a, b, preferred_element_type=F32)


def _dot_nt(a, b):
    return lax.dot_general(a, b, (((1,), (1,)), ((), ())), preferred_element_type=F32)


def _modulated_norm(x, mod, nw):
    ms = jnp.mean(x * x, axis=-1, keepdims=True)
    gain = nw * (1.0 + mod[1:2, :])
    return x * lax.rsqrt(ms + EPS) * gain + mod[0:1, :]


def _params(*sem):
    return pltpu.CompilerParams(dimension_semantics=sem, vmem_limit_bytes=VMEM_LIMIT)


def _resident(shape):
    nd = len(shape)
    return pl.BlockSpec(shape, lambda *_: (0,) * nd, pipeline_mode=pl.Buffered(1))


def _ada_kernel(c_ref, w_ref, b_ref, o_ref):
    cond = _silu(c_ref[...])
    o_ref[0] = _dot(cond, w_ref[0]) + b_ref[0]


def _ada_mod(c, ada_w, ada_b):
    B, D = c.shape
    depth = ada_w.shape[0]
    out = pl.pallas_call(
        _ada_kernel,
        grid=(depth,),
        in_specs=[
            pl.BlockSpec((B, D), lambda i: (0, 0)),
            pl.BlockSpec((1, D, 3 * D), lambda i: (i, 0, 0)),
            pl.BlockSpec((1, 1, 3 * D), lambda i: (i, 0, 0)),
        ],
        out_specs=pl.BlockSpec((1, B, 3 * D), lambda i: (i, 0, 0)),
        out_shape=jax.ShapeDtypeStruct((depth, B, 3 * D), F32),
        compiler_params=_params("arbitrary"),
        name="ada_mod",
    )(c, ada_w, ada_b.reshape(depth, 1, 3 * D))
    return out.reshape(depth, B, 3, D)


def _gdn_in_kernel(x_ref, mod_ref, nw_ref, w_ref, wba_ref, wbat_ref, cw_ref,
                   alog_ref, dtb_ref, alogc_ref, dtbc_ref,
                   q_ref, k_ref, kt_ref, v_ref, z_ref, bg_ref, gt_ref, cbuf_ref, *, tm):
    s = pl.program_id(1)
    tail = SUBLANES
    nchunk = tm // CHUNK

    @pl.when(s == 0)
    def _():
        cbuf_ref[0:tail, :] = jnp.zeros((tail, GDN_CONV_CH), F32)

    h = _modulated_norm(x_ref[0], mod_ref[0, 0], nw_ref[...])
    hb = h.astype(BF16)

    qscale = GDN_HEAD_DIM ** -0.5
    for c in range(GDN_CONV_CH // CONV_GROUP):
        lo = c * CONV_GROUP
        pre = _dot(hb, w_ref[:, lo:lo + CONV_GROUP])
        cbuf_ref[tail:tail + tm, lo:lo + CONV_GROUP] = pre
        acc = cw_ref[CONV_WIDTH - 1:CONV_WIDTH, lo:lo + CONV_GROUP] * pre
        for j in range(CONV_WIDTH - 1):
            off = tail - (CONV_WIDTH - 1) + j
            acc = acc + cw_ref[j:j + 1, lo:lo + CONV_GROUP] * cbuf_ref[off:off + tm, lo:lo + CONV_GROUP]
        y = _silu(acc)
        if lo < 2 * GDN_QK_WIDTH:
            for hh in range(CONV_GROUP // GDN_HEAD_DIM):
                seg = y[:, hh * GDN_HEAD_DIM:(hh + 1) * GDN_HEAD_DIM]
                inv = lax.rsqrt(jnp.sum(seg * seg, axis=-1, keepdims=True) + EPS)
                col = lo + hh * GDN_HEAD_DIM
                if col < GDN_QK_WIDTH:
                    q_ref[0, :, col:col + GDN_HEAD_DIM] = (seg * inv * qscale).astype(BF16)
                else:
                    col -= GDN_QK_WIDTH
                    kn = seg * inv
                    k_ref[0, :, col:col + GDN_HEAD_DIM] = kn.astype(BF16)
                    knt = kn.T.astype(BF16)
                    for cc in range(nchunk):
                        kt_ref[0, cc, col:col + GDN_HEAD_DIM, :] = knt[:, cc * CHUNK:(cc + 1) * CHUNK]
        else:
            col = lo - 2 * GDN_QK_WIDTH
            v_ref[0, :, col:col + CONV_GROUP] = y.astype(BF16)
    cbuf_ref[0:tail, :] = cbuf_ref[tm:tm + tail, :]

    for c in range(GDN_V_WIDTH // CONV_GROUP):
        lo = c * CONV_GROUP
        z = _dot(hb, w_ref[:, GDN_CONV_CH + lo:GDN_CONV_CH + lo + CONV_GROUP])
        z_ref[0, :, lo:lo + CONV_GROUP] = z.astype(BF16)

    nh = GDN_V_HEADS
    ba = _dot(hb, wba_ref[...])
    beta = _sigmoid(ba[:, 0:nh])
    g = -jnp.exp(alog_ref[...]) * _softplus(ba[:, LANES:LANES + nh] + dtb_ref[...])
    row = lax.broadcasted_iota(jnp.int32, (tm, tm), 0)
    col = lax.broadcasted_iota(jnp.int32, (tm, tm), 1)
    same = (row // CHUNK) == (col // CHUNK)
    tri_lo = jnp.where(same & (col <= row), 1.0, 0.0).astype(BF16)
    tri_up = jnp.where(same & (row <= col), 1.0, 0.0).astype(BF16)
    gc = sum(_dot(tri_lo, p) for p in _split3(g))
    bg_ref[0, :, 0:nh] = beta
    bg_ref[0, :, nh:2 * nh] = gc
    bat = _dot_nt(wbat_ref[...], hb)
    g_t = -jnp.exp(alogc_ref[...]) * _softplus(bat + dtbc_ref[...])
    gct = sum(_dot(p, tri_up) for p in _split3(g_t))
    for cc in range(nchunk):
        gt_ref[0, cc] = gct[:, cc * CHUNK:(cc + 1) * CHUNK]


def _gdn_in(x, mod, nw, w_main, w_ba, w_at, conv_w, a_log, dt_bias):
    B, S, D = x.shape
    tm = TM_GDN_IN
    nh = GDN_V_HEADS
    kern = functools.partial(_gdn_in_kernel, tm=tm)
    tok = lambda width: pl.BlockSpec((1, tm, width), lambda b, s: (b, s, 0))
    return pl.pallas_call(
        kern,
        grid=(B, S // tm),
        in_specs=[
            tok(D),
            pl.BlockSpec((1, 1, 3, D), lambda b, s: (b, 0, 0, 0)),
            _resident((1, D)),
            _resident(w_main.shape),
            _resident(w_ba.shape),
            _resident(w_at.shape),
            _resident(conv_w.shape),
            _resident((1, nh)), _resident((1, nh)), _resident((nh, 1)), _resident((nh, 1)),
        ],
        out_specs=[
            tok(GDN_QK_WIDTH), tok(GDN_QK_WIDTH),
            pl.BlockSpec((1, tm // CHUNK, GDN_QK_WIDTH, CHUNK), lambda b, s: (b, s, 0, 0)),
            tok(GDN_V_WIDTH), tok(GDN_V_WIDTH),
            tok(2 * nh),
            pl.BlockSpec((1, tm // CHUNK, nh, CHUNK), lambda b, s: (b, s, 0, 0)),
        ],
        out_shape=[
            jax.ShapeDtypeStruct((B, S, GDN_QK_WIDTH), BF16),
            jax.ShapeDtypeStruct((B, S, GDN_QK_WIDTH), BF16),
            jax.ShapeDtypeStruct((B, S // CHUNK, GDN_QK_WIDTH, CHUNK), BF16),
            jax.ShapeDtypeStruct((B, S, GDN_V_WIDTH), BF16),
            jax.ShapeDtypeStruct((B, S, GDN_V_WIDTH), BF16),
            jax.ShapeDtypeStruct((B, S, 2 * nh), F32),
            jax.ShapeDtypeStruct((B, S // CHUNK, nh, CHUNK), F32),
        ],
        scratch_shapes=[pltpu.VMEM((tm + SUBLANES, GDN_CONV_CH), F32)],
        compiler_params=_params("arbitrary", "arbitrary"),
        name="gdn_in",
    )(x, mod.reshape(B, 1, 3, D), nw.reshape(1, D), w_main, w_ba, w_at, conv_w,
      a_log.reshape(1, nh), dt_bias.reshape(1, nh), a_log.reshape(nh, 1), dt_bias.reshape(nh, 1))


def _delta_kernel(q_ref, k_ref, kt_ref, v_ref, z_ref, bg_ref, gt_ref, nw_ref, o_ref, s_ref, *, tc):
    s = pl.program_id(1)
    C = CHUNK
    dh = GDN_HEAD_DIM
    nqk = GDN_QK_HEADS
    nv = GDN_V_HEADS
    rep = nv // nqk
    heads = range(nv)

    @pl.when(s == 0)
    def _():
        s_ref[...] = jnp.zeros(s_ref.shape, F32)

    row = lax.broadcasted_iota(jnp.int32, (C, C), 0)
    col = lax.broadcasted_iota(jnp.int32, (C, C), 1)
    lower = row >= col
    strict = row > col
    eye = jnp.where(row == col, 1.0, 0.0).astype(F32)
    nw = nw_ref[...]

    nper = DELTA_CHUNKS_PER_TRIP
    units = [(t, h) for t in range(nper) for h in heads]

    def trip(i, _):
        cs = [i * nper + t for t in range(nper)]
        rows = [pl.ds(pl.multiple_of(c * C, C), C) for c in cs]
        bg = [bg_ref[0, rows[t], :] for t in range(nper)]
        gt = [gt_ref[0, cs[t]] for t in range(nper)]
        groups = [(t, j) for t in range(nper) for j in range(nqk)]
        qb = {(t, j): q_ref[0, rows[t], j * dh:(j + 1) * dh] for t, j in groups}
        kb = {(t, j): k_ref[0, rows[t], j * dh:(j + 1) * dh] for t, j in groups}
        ktb = {(t, j): kt_ref[0, cs[t], j * dh:(j + 1) * dh, :] for t, j in groups}
        kq = {g: _dot(jnp.concatenate([kb[g], qb[g]], axis=0), ktb[g]) for g in groups}
        grp = {(t, h): (t, h // rep) for t, h in units}

        beta = {(t, h): bg[t][:, h:h + 1] for t, h in units}
        gc = {(t, h): bg[t][:, nv + h:nv + h + 1] for t, h in units}
        gr = {(t, h): gt[t][h:h + 1, :] for t, h in units}
        g_last = {u: gr[u][:, C - 1:C] for u in units}
        decay = {u: jnp.exp(jnp.where(lower, gc[u] - gr[u], -jnp.inf)) for u in units}
        egc = {u: jnp.exp(gc[u]) for u in units}

        X = {u: -(jnp.where(strict, kq[grp[u]][:C] * decay[u], 0.0) * beta[u]) for u in units}
        negL = X
        P = {u: eye + X[u] for u in units}
        Xb = {u: X[u].astype(BF16) for u in units}
        X = {u: _dot(Xb[u], Xb[u]) for u in units}
        p = 4
        while p < C // 2:
            Xb = {u: X[u].astype(BF16) for u in units}
            PX = {u: _dot(jnp.concatenate([P[u], X[u]], axis=0).astype(BF16), Xb[u]) for u in units}
            P = {u: P[u] + PX[u][:C] for u in units}
            X = {u: PX[u][C:] for u in units}
            p *= 2
        P = {u: P[u] + _dot(P[u].astype(BF16), X[u].astype(BF16)) for u in units}
        Tb, rest = {}, {}
        for u in units:
            Tb[u] = P[u].astype(BF16)
            nl_hi = negL[u].astype(BF16)
            nl_lo = (negL[u] - nl_hi.astype(F32)).astype(BF16)
            hl = _dot(jnp.concatenate([nl_hi, nl_lo], axis=0), Tb[u])
            rest[u] = (eye - Tb[u].astype(F32)) + (hl[:C] + hl[C:])
        P = {u: Tb[u].astype(F32) + _dot(Tb[u], rest[u].astype(BF16)) for u in units}

        uw = {}
        for t, h in units:
            u = (t, h)
            vf = v_ref[0, rows[t], h * dh:(h + 1) * dh].astype(F32)
            kf = kb[grp[u]].astype(F32)
            rhs = jnp.concatenate([vf * beta[u], kf * (beta[u] * egc[u])], axis=1).astype(BF16)
            uw[u] = _dot(P[u].astype(BF16), rhs)
        wq = {u: jnp.concatenate([uw[u][:, dh:], qb[grp[u]].astype(F32) * egc[u]], axis=0).astype(BF16)
              for u in units}
        ak_lhs = {}
        for u in units:
            attn = (kq[grp[u]][C:] * decay[u]).astype(BF16)
            kdt = (ktb[grp[u]].astype(F32) * jnp.exp(g_last[u] - gr[u])).astype(BF16)
            ak_lhs[u] = jnp.concatenate([attn, kdt], axis=0)
        for t in range(nper):
            S = [s_ref[h] for h in heads]
            r = [_dot(wq[(t, h)], S[h].astype(BF16)) for h in heads]
            vnb = [(uw[(t, h)][:, :dh] - r[h][:C]).astype(BF16) for h in heads]
            ak = [_dot(ak_lhs[(t, h)], vnb[h]) for h in heads]
            for h in heads:
                s_ref[h] = S[h] * jnp.exp(g_last[(t, h)]) + ak[h][C:]
            for h in heads:
                o = r[h][C:] + ak[h][:C]
                var = jnp.mean(o * o, axis=-1, keepdims=True)
                zf = z_ref[0, rows[t], h * dh:(h + 1) * dh].astype(F32)
                o_ref[0, rows[t], h * dh:(h + 1) * dh] = (o * lax.rsqrt(var + EPS) * nw * _silu(zf)).astype(BF16)
        return 0

    lax.fori_loop(0, tc // (C * nper), trip, 0)


def _gdn_delta(q, k, kt, v, z, bg, gt, nw):
    B, S, _ = q.shape
    tc = TC_DELTA
    nv = GDN_V_HEADS
    dh = GDN_HEAD_DIM
    kern = functools.partial(_delta_kernel, tc=tc)
    tok = lambda width: pl.BlockSpec((1, tc, width), lambda b, s: (b, s, 0))
    return pl.pallas_call(
        kern,
        grid=(B, S // tc),
        in_specs=[
            tok(GDN_QK_WIDTH), tok(GDN_QK_WIDTH),
            pl.BlockSpec((1, tc // CHUNK, GDN_QK_WIDTH, CHUNK), lambda b, s: (b, s, 0, 0)),
            tok(GDN_V_WIDTH), tok(GDN_V_WIDTH),
            tok(2 * nv),
            pl.BlockSpec((1, tc // CHUNK, nv, CHUNK), lambda b, s: (b, s, 0, 0)),
            pl.BlockSpec((1, dh), lambda b, s: (0, 0)),
        ],
        out_specs=tok(GDN_V_WIDTH),
        out_shape=jax.ShapeDtypeStruct((B, S, GDN_V_WIDTH), BF16),
        scratch_shapes=[pltpu.VMEM((nv, dh, dh), F32)],
        compiler_params=_params("arbitrary", "arbitrary"),
        name="gdn_delta",
    )(q, k, kt, v, z, bg, gt, nw.reshape(1, dh))


def _fox_in_kernel(x_ref, mod_ref, nw_ref, wt_ref, wz_ref, wft_ref, fb_ref, qw_ref, kw_ref,
                   qt_ref, kt_ref, vt_ref, z_ref, cum_ref, carry_ref, *, tm):
    s = pl.program_id(1)
    W = FOX_WIDTH
    dh = FOX_HEAD_DIM

    @pl.when(s == 0)
    def _():
        carry_ref[...] = jnp.zeros(carry_ref.shape, F32)

    h = _modulated_norm(x_ref[0], mod_ref[0, 0], nw_ref[...])
    hb = h.astype(BF16)

    grp = FOX_GROUP
    for g in range(3 * W // grp):
        blk = _dot_nt(wt_ref[g * grp:(g + 1) * grp, :], hb)
        r0 = (g * grp) % W
        if g * grp < 2 * W:
            is_q = g * grp < W
            wcol = qw_ref[...] if is_q else kw_ref[...]
            dst = qt_ref if is_q else kt_ref
            for hh in range(grp // dh):
                seg = blk[hh * dh:(hh + 1) * dh]
                inv = lax.rsqrt(jnp.mean(seg * seg, axis=0, keepdims=True) + EPS)
                dst[0, r0 + hh * dh:r0 + (hh + 1) * dh, :] = (seg * inv * wcol).astype(BF16)
        else:
            vt_ref[0, r0:r0 + grp, :] = blk.astype(BF16)
    z_ref[0] = _dot(hb, wz_ref[...]).astype(BF16)

    xf = _dot_nt(wft_ref[...], hb) + fb_ref[...]
    log_f = -_softplus(-xf)
    row = lax.broadcasted_iota(jnp.int32, (tm, tm), 0)
    col = lax.broadcasted_iota(jnp.int32, (tm, tm), 1)
    tri_up = jnp.where(row <= col, 1.0, 0.0).astype(BF16)
    cum = sum(_dot(p, tri_up) for p in _split3(log_f)) + carry_ref[...]
    cum_ref[0] = cum
    carry_ref[...] = cum[:, tm - 1:tm]


def _fox_in(x, mod, nw, w_qkv_t, w_z, w_ft, f_bias, qw, kw):
    B, S, D = x.shape
    tm = TM_PROJ
    H = FOX_HEADS
    dh = FOX_HEAD_DIM
    W = FOX_WIDTH
    kern = functools.partial(_fox_in_kernel, tm=tm)
    feat_major = pl.BlockSpec((1, W, tm), lambda b, s: (b, 0, s))
    fm_shape = jax.ShapeDtypeStruct((B, W, S), BF16)
    return pl.pallas_call(
        kern,
        grid=(B, S // tm),
        in_specs=[
            pl.BlockSpec((1, tm, D), lambda b, s: (b, s, 0)),
            pl.BlockSpec((1, 1, 3, D), lambda b, s: (b, 0, 0, 0)),
            _resident((1, D)),
            _resident(w_qkv_t.shape),
            _resident(w_z.shape),
            _resident(w_ft.shape),
            _resident((H, 1)),
            _resident((dh, 1)),
            _resident((dh, 1)),
        ],
        out_specs=[
            feat_major, feat_major, feat_major,
            pl.BlockSpec((1, tm, W), lambda b, s: (b, s, 0)),
            pl.BlockSpec((1, H, tm), lambda b, s: (b, 0, s)),
        ],
        out_shape=[
            fm_shape, fm_shape, fm_shape,
            jax.ShapeDtypeStruct((B, S, W), BF16),
            jax.ShapeDtypeStruct((B, H, S), F32),
        ],
        scratch_shapes=[pltpu.VMEM((H, 1), F32)],
        compiler_params=_params("arbitrary", "arbitrary"),
        name="fox_in",
    )(x, mod.reshape(B, 1, 3, D), nw.reshape(1, D), w_qkv_t, w_z, w_ft, f_bias.reshape(H, 1),
      qw.reshape(dh, 1), kw.reshape(dh, 1))


def _fox_attn_kernel(qt_ref, kt_ref, vt_ref, z_ref, cum_ref, o_ref,
                     qa_ref, ka_ref, va_ref, r_ref, m_ref, acc_ref, st_ref, p_ref, *, tq, seq):
    hp = pl.program_id(1)
    dh = FOX_HEAD_DIM
    pair = LANES // dh
    nq = seq // tq
    nsplit = 3
    heads = range(pair)
    kidx = lax.broadcasted_iota(jnp.int32, (tq, tq), 0)
    qidx = lax.broadcasted_iota(jnp.int32, (tq, tq), 1)
    causal = kidx <= qidx

    aug_row = lax.broadcasted_iota(jnp.int32, (BF16_ROWS, seq), 0)
    pick = jnp.where(aug_row < nsplit, 1.0, 0.0).astype(BF16)
    one_row = jnp.where(aug_row < 1, 1.0, 0.0).astype(BF16)
    for e in heads:
        qa_ref[e, 0:dh, :] = qt_ref[0, e * dh:(e + 1) * dh, :]
        qa_ref[e, dh:dh + BF16_ROWS, :] = pick
        qa_ref[e, dh + BF16_ROWS:, :] = jnp.zeros((LANES - dh - BF16_ROWS, seq), BF16)
        va_ref[e, 0:dh, :] = vt_ref[0, e * dh:(e + 1) * dh, :]
        va_ref[e, dh:, :] = one_row
        crow = cum_ref[0, pl.ds(pair * hp + e, 1), :]
        firsts = [jnp.broadcast_to(crow[:, j * tq:j * tq + 1], (1, tq)) for j in range(nq)]
        for j in range(nq):
            r_ref[e, j] = firsts[j] * (-LOG2E)
        rel = (crow - jnp.concatenate(firsts, axis=1)) * (-LOG2E)
        parts = [p.astype(F32) for p in _split3(rel)]
        btile = jnp.concatenate(parts + [jnp.zeros((SUBLANES - nsplit, seq), F32)], axis=0)
        for blk in range(seq // KT_BLOCK):
            sl = slice(blk * KT_BLOCK, (blk + 1) * KT_BLOCK)
            top = jnp.concatenate([kt_ref[0, e * dh:(e + 1) * dh, sl].astype(F32), btile[:, sl],
                                   jnp.zeros((LANES - dh - SUBLANES, KT_BLOCK), F32)], axis=0)
            ka_ref[e, sl, :] = top.T.astype(BF16)

    m_ref[...] = jnp.full(m_ref.shape, -jnp.inf, F32)
    acc_ref[...] = jnp.zeros(acc_ref.shape, F32)
    p_ref[...] = jnp.zeros(p_ref.shape, BF16)

    def scores(e, qi, kj):
        return _dot(ka_ref[e, pl.ds(pl.multiple_of(kj * tq, tq), tq), :],
                    qa_ref[e, :, pl.ds(pl.multiple_of(qi * tq, tq), tq)])

    def weighted_values(e, kj, p):
        return _dot(va_ref[e, :, pl.ds(pl.multiple_of(kj * tq, tq), tq)], p)

    def softmax_update(e, qi, kj, read_scores):
        off = r_ref[e, kj]
        m_rel = m_ref[e, qi] - off
        m_new = jnp.maximum(m_rel, jnp.max(read_scores(), axis=0, keepdims=True))
        m_ref[e, qi] = m_new + off
        return jnp.exp2(m_rel - m_new), jnp.exp2(read_scores() - m_new).astype(BF16)

    def mask_diagonal(st, masked):
        return jnp.where(causal, st, -jnp.inf) if masked else st

    def simple_step(pr, masked):
        qi, kj = pr
        for e in heads:
            st = mask_diagonal(scores(e, qi, kj), masked)
            alpha, p = softmax_update(e, qi, kj, lambda st=st: st)
            acc_ref[e, qi] = alpha * acc_ref[e, qi] + weighted_values(e, kj, p)

    def sweep(first, advance, npairs, masked, unroll):
        n_main = (npairs // unroll) * unroll
        cur = first
        if n_main:
            second = advance(*first)
            for e in heads:
                st_ref[e, 0] = scores(e, *first)
                st_ref[e, 1] = scores(e, *second)

            def body(_, carry):
                prev, cur, nxt, a_prev = carry
                for i in range(unroll):
                    nxt2 = advance(*nxt)
                    new_a = []
                    for e in heads:
                        pv_prev = weighted_values(e, prev[1], p_ref[e, (i - 1) % P_RING])
                        st_ref[e, (i + 2) % ST_RING] = scores(e, *nxt2)
                        alpha, p = softmax_update(
                            e, cur[0], cur[1], lambda e=e, i=i: mask_diagonal(st_ref[e, i % ST_RING], masked))
                        p_ref[e, i % P_RING] = p
                        acc_ref[e, prev[0]] = a_prev[e] * acc_ref[e, prev[0]] + pv_prev
                        new_a.append(alpha)
                    prev, cur, nxt, a_prev = cur, nxt, nxt2, tuple(new_a)
                return prev, cur, nxt, a_prev

            ones = tuple(jnp.ones((1, tq), F32) for _ in heads)
            spare = (jnp.int32(nq), jnp.int32(0))
            prev, cur, _, a_prev = lax.fori_loop(0, n_main // unroll, body, (spare, first, second, ones))
            for e in heads:
                pv_prev = weighted_values(e, prev[1], p_ref[e, (unroll - 1) % P_RING])
                acc_ref[e, prev[0]] = a_prev[e] * acc_ref[e, prev[0]] + pv_prev

        def tail(_, pr):
            simple_step(pr, masked)
            return advance(*pr)

        lax.fori_loop(0, npairs - n_main, tail, cur)

    def next_off_diagonal(qi, kj):
        k2 = kj + 1
        wrap = k2 >= qi
        return jnp.where(wrap, jnp.minimum(qi + 1, nq - 1), qi), jnp.where(wrap, 0, k2)

    def next_diagonal(qi, kj):
        nxt = jnp.minimum(qi + 1, nq - 1)
        return nxt, nxt

    zero = jnp.int32(0)
    sweep((jnp.int32(min(1, nq - 1)), zero), next_off_diagonal, nq * (nq - 1) // 2, False, ATTN_UNROLL)
    sweep((zero, zero), next_diagonal, nq, True, ATTN_UNROLL_DIAG)

    per_trip = FINISH_UNROLL if nq % FINISH_UNROLL == 0 else 1

    def finish(i, _):
        for t in range(per_trip):
            qi = i * per_trip + t
            accs = [acc_ref[e, qi] for e in heads]
            o = jnp.concatenate([a[0:dh] * (1.0 / a[dh:dh + 1]) for a in accs], axis=0).T
            rows = pl.ds(pl.multiple_of(qi * tq, tq), tq)
            zf = z_ref[0, rows, :].astype(F32)
            o_ref[0, rows, :] = (o * _silu(zf)).astype(BF16)
        return 0

    lax.fori_loop(0, nq // per_trip, finish, 0)


def _fox_attn(qt, kt, vt, z, cum):
    B, W, S = qt.shape
    H = FOX_HEADS
    dh = FOX_HEAD_DIM
    pair = LANES // dh
    tq = TQ_ATTN
    nq = S // tq
    kern = functools.partial(_fox_attn_kernel, tq=tq, seq=S)
    fm = pl.BlockSpec((1, LANES, S), lambda b, h: (b, h, 0))
    return pl.pallas_call(
        kern,
        grid=(B, H // pair),
        in_specs=[
            fm, fm, fm,
            pl.BlockSpec((1, S, LANES), lambda b, h: (b, 0, h)),
            pl.BlockSpec((1, H, S), lambda b, h: (b, 0, 0)),
        ],
        out_specs=pl.BlockSpec((1, S, LANES), lambda b, h: (b, 0, h)),
        out_shape=jax.ShapeDtypeStruct((B, S, W), BF16),
        scratch_shapes=[
            pltpu.VMEM((pair, LANES, S), BF16),
            pltpu.VMEM((pair, S, LANES), BF16),
            pltpu.VMEM((pair, dh + BF16_ROWS, S), BF16),
            pltpu.VMEM((pair, nq, 1, tq), F32),
            pltpu.VMEM((pair, nq, 1, tq), F32),
            pltpu.VMEM((pair, nq + 1, dh + BF16_ROWS, tq), F32),
            pltpu.VMEM((pair, ST_RING, tq, tq), F32),
            pltpu.VMEM((pair, P_RING, tq, tq), BF16),
        ],
        compiler_params=_params("arbitrary", "arbitrary"),
        name="fox_attn",
    )(qt, kt, vt, z, cum)


def _out_kernel(o_ref, x_ref, mod_ref, w_ref, fnw_ref, y_ref, *, final):
    y = _dot(o_ref[0], w_ref[...])
    xn = x_ref[0] + mod_ref[0, 0][2:3, :] * y
    if final:
        ms = jnp.mean(xn * xn, axis=-1, keepdims=True)
        xn = xn * lax.rsqrt(ms + EPS) * fnw_ref[...]
    y_ref[0] = xn


def _out_proj(o, x, mod, w_out, fnw, final):
    B, S, D = x.shape
    width = o.shape[-1]
    tm = TM_PROJ
    kern = functools.partial(_out_kernel, final=final)
    return pl.pallas_call(
        kern,
        grid=(B, S // tm),
        in_specs=[
            pl.BlockSpec((1, tm, width), lambda b, s: (b, s, 0)),
            pl.BlockSpec((1, tm, D), lambda b, s: (b, s, 0)),
            pl.BlockSpec((1, 1, 3, D), lambda b, s: (b, 0, 0, 0)),
            _resident(w_out.shape),
            _resident((1, D)),
        ],
        out_specs=pl.BlockSpec((1, tm, D), lambda b, s: (b, s, 0)),
        out_shape=jax.ShapeDtypeStruct((B, S, D), F32),
        compiler_params=_params("arbitrary", "arbitrary"),
        name="out_proj",
    )(o, x, mod.reshape(B, 1, 3, D), w_out, fnw.reshape(1, D))


def kernel(x, c, norm_w, ada_w, ada_b, a_w_in, a_conv_w, a_A_log, a_dt_bias, a_norm_w, a_w_out,
           b_w_in, b_f_bias, b_qn_w, b_kn_w, b_w_out, final_norm_w):
    B, S, D = x.shape
    assert D == D_MODEL and S % max(TM_PROJ, TM_GDN_IN, TC_DELTA, TQ_ATTN) == 0
    mods = _ada_mod(c, ada_w, ada_b)
    nh = GDN_V_HEADS
    for i in range(DEPTH):
        j = i // 2
        final = i == DEPTH - 1
        if i % 2 == 0:
            w_in = a_w_in[j]
            w_main = w_in[:, :GDN_CONV_CH + GDN_V_WIDTH].astype(BF16)
            w_b = w_in[:, GDN_CONV_CH + GDN_V_WIDTH:GDN_CONV_CH + GDN_V_WIDTH + nh]
            w_a = w_in[:, GDN_CONV_CH + GDN_V_WIDTH + nh:]
            pad = jnp.zeros((D, LANES - nh), F32)
            w_ba = jnp.concatenate([w_b, pad, w_a, pad], axis=1).astype(BF16)
            w_at = w_a.T.astype(BF16)
            q, k, kt, v, z, bg, gt = _gdn_in(x, mods[i], norm_w[i], w_main, w_ba, w_at, a_conv_w[j],
                                              a_A_log[j], a_dt_bias[j])
            o = _gdn_delta(q, k, kt, v, z, bg, gt, a_norm_w[j])
            w_out = a_w_out[j].astype(BF16)
        else:
            w_in = b_w_in[j]
            w_qkv_t = w_in[:, :3 * FOX_WIDTH].T.astype(BF16)
            w_z = w_in[:, 3 * FOX_WIDTH:4 * FOX_WIDTH].astype(BF16)
            w_ft = w_in[:, 4 * FOX_WIDTH:].T.astype(BF16)
            qw = b_qn_w[j] * (FOX_HEAD_DIM ** -0.5 * LOG2E)
            qt, kt, vt, z, cum = _fox_in(x, mods[i], norm_w[i], w_qkv_t, w_z, w_ft, b_f_bias[j], qw, b_kn_w[j])
            o = _fox_attn(qt, kt, vt, z, cum)
            w_out = b_w_out[j].astype(BF16)
        x = _out_proj(o, x, mods[i], w_out, final_norm_w, final)
    return x
```
